```python
import functools
import jax
import jax.numpy as jnp
from jax import lax
import numpy as np

D_MODEL = 1024
BATCH = 4
SEQ = 4096
DEPTH = 2
DEC_BATCH = 32
DEC_SEQ = 4
PAST_LEN = 16384
PAGE_SIZE = 128

LRU_WIDTH = D_MODEL // 2
LRU_BLOCKS = 8
LRU_BLOCK = LRU_WIDTH // LRU_BLOCKS
CONV_W = 4
LRU_C = 8.0
N_HEADS = 8
HEAD_DIM = 64
ATT_WIDTH = N_HEADS * HEAD_DIM
N_KV = 2
Q_PER_KV = N_HEADS // N_KV
ROPE_DIM = HEAD_DIM // 4
ROPE_THETA = 500000.0
IDX_HEADS = 4
IDX_DIM = 64
TOP_K = 256
Q_BLOCK = 128
D_FF = 2816
N_MOD = 9
RMS_EPS = 1e-6
IN_SIZES = (LRU_WIDTH, LRU_WIDTH, ATT_WIDTH, N_KV * HEAD_DIM, N_KV * HEAD_DIM, IDX_HEADS * IDX_DIM, IDX_DIM, IDX_HEADS)
IN_COLS = sum(IN_SIZES)

kernel_name = 'hymba_rglru_dsa_macaron_decoder_step'

F32 = jnp.float32


def rms_norm(x, g):
    x32 = x.astype(F32)
    y = x32 * lax.rsqrt(jnp.mean(x32 * x32, axis=-1, keepdims=True) + RMS_EPS)
    return (y * g.astype(F32)).astype(x.dtype)


def swiglu(h, w_gu, w_dn):
    g, u = jnp.split(h @ w_gu, 2, axis=-1)
    return (jax.nn.silu(g) * u) @ w_dn


def rope_partial(x, pos):
    half = ROPE_DIM // 2
    freqs = ROPE_THETA ** (-(jnp.arange(half, dtype=F32) / half))
    ang = pos.astype(F32)[:, None] * freqs[None, :]
    cos = jnp.cos(ang)[None, :, None, :]
    sin = jnp.sin(ang)[None, :, None, :]
    xr = x[..., :ROPE_DIM].astype(F32)
    x1, x2 = xr[..., :half], xr[..., half:]
    rot = jnp.concatenate([x1 * cos - x2 * sin, x2 * cos + x1 * sin], axis=-1).astype(x.dtype)
    return jnp.concatenate([rot, x[..., ROPE_DIM:]], axis=-1)


def split_cols(t):
    parts, start = [], 0
    for n in IN_SIZES:
        parts.append(t[..., start:start + n])
        start += n
    return parts


def project(h, w_in, pos):
    B, T = h.shape[:2]
    xl, gl, q, k, v, qi, ki, wi = split_cols(h @ w_in)
    q = rope_partial(q.reshape(B, T, N_HEADS, HEAD_DIM), pos)
    k = rope_partial(k.reshape(B, T, N_KV, HEAD_DIM), pos)
    v = v.reshape(B, T, N_KV, HEAD_DIM)
    qi = rope_partial(qi.reshape(B, T, IDX_HEADS, IDX_DIM), pos)
    ki = rope_partial(ki.reshape(B, T, 1, IDX_DIM), pos)[:, :, 0]
    return xl, gl, q, k, v, qi, ki, wi


def causal_conv(x, buf, w, b):
    T = x.shape[1]
    xp = jnp.concatenate([buf.astype(x.dtype), x], axis=1)
    out = b
    for j in range(CONV_W):
        out = out + xp[:, j:j + T] * w[j]
    return out, xp[:, -(CONV_W - 1):]


def rg_lru(xc, h0, wa, ba, wx, bx, lam):
    B, T, _ = xc.shape
    x32 = xc.astype(F32)
    xb = x32.reshape(B, T, LRU_BLOCKS, LRU_BLOCK)
    r = jax.nn.sigmoid(jnp.einsum('btni,nij->btnj', xb, wa.astype(F32)).reshape(B, T, LRU_WIDTH) + ba.astype(F32))
    i = jax.nn.sigmoid(jnp.einsum('btni,nij->btnj', xb, wx.astype(F32)).reshape(B, T, LRU_WIDTH) + bx.astype(F32))
    log_a = -LRU_C * r * jax.nn.softplus(-lam.astype(F32))
    a = jnp.exp(log_a)
    bt = jnp.sqrt(-jnp.expm1(2.0 * log_a)) * (i * x32)

    def step(h, ab):
        h = ab[0] * h + ab[1]
        return h, h

    hT, hs = lax.scan(step, h0, (jnp.swapaxes(a, 0, 1), jnp.swapaxes(bt, 0, 1)))
    return jnp.swapaxes(hs, 0, 1), hT


def select_keys(qi, wi, ki, q_pos, n_sel):
    dots = jnp.einsum('bthd,bsd->bths', qi, ki).astype(F32) * (IDX_DIM ** -0.5)
    score = jnp.einsum('bths,bth->bts', jax.nn.relu(dots), wi.astype(F32) * (IDX_HEADS ** -0.5))
    causal = jnp.arange(ki.shape[1])[None, :] <= q_pos[:, None]
    score = jnp.where(causal[None], score, -jnp.inf)
    _, idx = lax.top_k(score, n_sel)
    valid = idx <= q_pos[None, :, None]
    return idx, valid


def sparse_attend(q, k_sel, v_sel, valid):
    B, T = q.shape[:2]
    qg = q.reshape(B, T, N_KV, Q_PER_KV, HEAD_DIM)
    s = jnp.einsum('btkgd,btskd->btkgs', qg, k_sel).astype(F32) * (HEAD_DIM ** -0.5)
    s = jnp.where(valid[:, :, None, None, :], s, -jnp.inf)
    p = jax.nn.softmax(s, axis=-1).astype(v_sel.dtype)
    o = jnp.einsum('btkgs,btskd->btkgd', p, v_sel)
    return o.reshape(B, T, ATT_WIDTH)


def gather_rows(t, idx):
    return jax.vmap(lambda tt, ii: tt[ii])(t, idx)


def prompt_attention(q, k, v, qi, ki, wi):
    B, S = q.shape[:2]
    n_sel = min(TOP_K, S // 4)
    nblk = S // Q_BLOCK

    def blk(args):
        j, qb, qib, wib = args
        pos = j * Q_BLOCK + jnp.arange(Q_BLOCK, dtype=jnp.int32)
        idx, valid = select_keys(qib, wib, ki, pos, n_sel)
        flat = idx.reshape(B, Q_BLOCK * n_sel)
        ks = gather_rows(k, flat).reshape(B, Q_BLOCK, n_sel, N_KV, HEAD_DIM)
        vs = gather_rows(v, flat).reshape(B, Q_BLOCK, n_sel, N_KV, HEAD_DIM)
        return sparse_attend(qb, ks, vs, valid)

    def to_blocks(t):
        return jnp.moveaxis(t.reshape(B, nblk, Q_BLOCK, *t.shape[2:]), 1, 0)

    out = lax.map(blk, (jnp.arange(nblk, dtype=jnp.int32), to_blocks(q), to_blocks(qi), to_blocks(wi)))
    return jnp.moveaxis(out, 0, 1).reshape(B, S, ATT_WIDTH)


def sample_attention(q, k, v, qi, ki, wi, cache_k, cache_v, cache_kidx, page_table, layer):
    DB, DS = q.shape[:2]
    page = cache_k.shape[2]
    past_len = page_table.shape[1] * page
    n_sel = min(TOP_K, (past_len + DS) // 4)
    ki_past = cache_kidx[layer, page_table].reshape(DB, past_len, IDX_DIM).astype(ki.dtype)
    ki_all = jnp.concatenate([ki_past, ki], axis=1)
    q_pos = past_len + jnp.arange(DS, dtype=jnp.int32)
    idx, valid = select_keys(qi, wi, ki_all, q_pos, n_sel)
    in_past = (idx < past_len)[..., None, None]
    pidx = jnp.minimum(idx, past_len - 1)
    phys = page_table[jnp.arange(DB)[:, None, None], pidx // page]
    off = pidx % page
    nidx = jnp.clip(idx - past_len, 0, DS - 1).reshape(DB, DS * n_sel)
    k_new = gather_rows(k, nidx).reshape(DB, DS, n_sel, N_KV, HEAD_DIM)
    v_new = gather_rows(v, nidx).reshape(DB, DS, n_sel, N_KV, HEAD_DIM)
    ks = jnp.where(in_past, cache_k[layer, phys, off].astype(k.dtype), k_new)
    vs = jnp.where(in_past, cache_v[layer, phys, off].astype(v.dtype), v_new)
    return sparse_attend(q, ks, vs, valid)


def layer_forward(x, c, pos, lw, conv_buf, h0, attn_fn):
    (w_ada_l, b_ada_l, g, w_gu, w_dn, w_in_l, cw, cb, wa, ba, wx, bx, lam, w_o) = lw
    mod = (c @ w_ada_l + b_ada_l).reshape(c.shape[0], N_MOD, -1)
    sh1, sc1, gt1, sh2, sc2, gt2, sh3, sc3, gt3 = [mod[:, m][:, None, :] for m in range(N_MOD)]
    f = swiglu(rms_norm(x, g[0]) * (1 + sc1) + sh1, w_gu[0], w_dn[0])
    x = x + 0.5 * gt1 * rms_norm(f, g[1])
    h = rms_norm(x, g[2]) * (1 + sc2) + sh2
    xl, gl, q, k, v, qi, ki, wi = project(h, w_in_l, pos)
    xc, new_buf = causal_conv(xl, conv_buf, cw, cb)
    hs, hT = rg_lru(xc, h0, wa, ba, wx, bx, lam)
    y_lru = (hs * jax.nn.gelu(gl.astype(F32))).astype(h.dtype)
    y_att = attn_fn(q, k, v, qi, ki, wi)
    y = jnp.concatenate([y_lru, y_att.astype(h.dtype)], axis=-1) @ w_o
    x = x + gt2 * rms_norm(y, g[3])
    f = swiglu(rms_norm(x, g[4]) * (1 + sc3) + sh3, w_gu[1], w_dn[1])
    x = x + 0.5 * gt3 * rms_norm(f, g[5])
    return x, (k, v, ki, hT, new_buf)


def setup_inputs(seed: int = 0) -> dict:
    key = jax.random.key(seed)
    ks = jax.random.split(key, 24)
    n_pages = PAST_LEN // PAGE_SIZE
    n_used = DEC_BATCH * n_pages
    n_pool = n_used + n_used // 4
    nrm = jax.random.normal
    page_table = jax.random.permutation(ks[0], n_pool)[:n_used].reshape(DEC_BATCH, n_pages).astype(jnp.int32)
    a0 = jax.random.uniform(ks[1], (DEPTH, LRU_WIDTH), F32, 0.9, 0.999)
    a = a0 ** (1.0 / LRU_C)
    lru_lambda = jnp.log(a) - jnp.log1p(-a)
    return {
        'x_prompt': nrm(ks[2], (BATCH, SEQ, D_MODEL), F32),
        'x_sample': nrm(ks[3], (DEC_BATCH, DEC_SEQ, D_MODEL), F32),
        'c_prompt': nrm(ks[4], (BATCH, D_MODEL), F32),
        'c_sample': nrm(ks[5], (DEC_BATCH, D_MODEL), F32),
        'cache_k': nrm(ks[6], (DEPTH, n_pool, PAGE_SIZE, N_KV, HEAD_DIM), F32),
        'cache_v': nrm(ks[7], (DEPTH, n_pool, PAGE_SIZE, N_KV, HEAD_DIM), F32),
        'cache_kidx': nrm(ks[8], (DEPTH, n_pool, PAGE_SIZE, IDX_DIM), F32),
        'state_h': 0.5 * nrm(ks[9], (DEPTH, DEC_BATCH, LRU_WIDTH), F32),
        'state_conv': nrm(ks[10], (DEPTH, DEC_BATCH, CONV_W - 1, LRU_WIDTH), F32),
        'page_table': page_table,
        'w_ada': 0.3 * D_MODEL ** -0.5 * nrm(ks[11], (DEPTH, D_MODEL, N_MOD * D_MODEL), F32),
        'b_ada': 0.02 * nrm(ks[12], (DEPTH, N_MOD * D_MODEL), F32),
        'norm_g': 1.0 + 0.05 * nrm(ks[13], (DEPTH, 6, D_MODEL), F32),
        'ffn_w_gu': D_MODEL ** -0.5 * nrm(ks[14], (DEPTH, 2, D_MODEL, 2 * D_FF), F32),
        'ffn_w_down': D_FF ** -0.5 * nrm(ks[15], (DEPTH, 2, D_FF, D_MODEL), F32),
        'w_in': D_MODEL ** -0.5 * nrm(ks[16], (DEPTH, D_MODEL, IN_COLS), F32),
        'conv_w': CONV_W ** -0.5 * nrm(ks[17], (DEPTH, CONV_W, LRU_WIDTH), F32),
        'conv_b': 0.02 * nrm(ks[18], (DEPTH, LRU_WIDTH), F32),
        'lru_wa': LRU_BLOCK ** -0.5 * nrm(ks[19], (DEPTH, LRU_BLOCKS, LRU_BLOCK, LRU_BLOCK), F32),
        'lru_ba': 0.02 * nrm(ks[20], (DEPTH, LRU_WIDTH), F32),
        'lru_wx': LRU_BLOCK ** -0.5 * nrm(ks[21], (DEPTH, LRU_BLOCKS, LRU_BLOCK, LRU_BLOCK), F32),
        'lru_bx': 0.02 * nrm(ks[22], (DEPTH, LRU_WIDTH), F32),
        'lru_lambda': lru_lambda,
        'w_out': D_MODEL ** -0.5 * nrm(ks[23], (DEPTH, D_MODEL, D_MODEL), F32),
    }


def reference(x_prompt, x_sample, c_prompt, c_sample, cache_k, cache_v, cache_kidx, state_h, state_conv, page_table,
              w_ada, b_ada, norm_g, ffn_w_gu, ffn_w_down, w_in, conv_w, conv_b, lru_wa, lru_ba, lru_wx, lru_bx,
              lru_lambda, w_out):
    B, S = x_prompt.shape[:2]
    DB, DS = x_sample.shape[:2]
    past_len = page_table.shape[1] * cache_k.shape[2]
    pos_p = jnp.arange(S, dtype=jnp.int32)
    pos_s = past_len + jnp.arange(DS, dtype=jnp.int32)
    yp, ys = x_prompt, x_sample
    kp, vp, kip, hp, cp = [], [], [], [], []
    kS, vS, kiS, hS, cS = [], [], [], [], []
    for l in range(DEPTH):
        lw = (w_ada[l], b_ada[l], norm_g[l], ffn_w_gu[l], ffn_w_down[l], w_in[l], conv_w[l], conv_b[l],
              lru_wa[l], lru_ba[l], lru_wx[l], lru_bx[l], lru_lambda[l], w_out[l])
        buf0 = jnp.zeros((B, CONV_W - 1, LRU_WIDTH), x_prompt.dtype)
        h0 = jnp.zeros((B, LRU_WIDTH), F32)
        yp, (k1, v1, ki1, h1, c1) = layer_forward(yp, c_prompt, pos_p, lw, buf0, h0, prompt_attention)
        attn_s = functools.partial(sample_attention, cache_k=cache_k, cache_v=cache_v, cache_kidx=cache_kidx,
                                   page_table=page_table, layer=l)
        ys, (k2, v2, ki2, h2, c2) = layer_forward(ys, c_sample, pos_s, lw, state_conv[l],
                                                  state_h[l].astype(F32), attn_s)
        kp.append(k1); vp.append(v1); kip.append(ki1); hp.append(h1.astype(x_prompt.dtype)); cp.append(c1)
        kS.append(k2); vS.append(v2); kiS.append(ki2); hS.append(h2.astype(state_h.dtype)); cS.append(c2.astype(state_conv.dtype))
    return (yp, ys, jnp.stack(kp), jnp.stack(vp), jnp.stack(kip), jnp.stack(hp), jnp.stack(cp),
            jnp.stack(kS), jnp.stack(vS), jnp.stack(kiS), jnp.stack(hS), jnp.stack(cS))
```

```python
import functools

import jax
import jax.numpy as jnp
from jax import lax
from jax.experimental import pallas as pl
from jax.experimental.pallas import tpu as pltpu

F32 = jnp.float32
I32 = jnp.int32
MXU_DTYPE = jnp.bfloat16

LRU_BLOCKS = 8
CONV_W = 4
LRU_C = 8.0
N_HEADS = 8
HEAD_DIM = 64
N_KV = 2
ROPE_DIM = HEAD_DIM // 4
ROPE_THETA = 500000.0
IDX_HEADS = 4
IDX_DIM = 64
TOP_K = 256
Q_BLOCK = 128
N_MOD = 9
RMS_EPS = 1e-6

LANES = 128
SUBLANES = 8
VMEM_LIMIT = 56 * 1024 * 1024
INT_MIN = -2147483648
NEG_INF = float("-inf")


def _cparams(n_axes):
    return pltpu.CompilerParams(dimension_semantics=("arbitrary",) * n_axes,
                                vmem_limit_bytes=VMEM_LIMIT)


def _rms(x, g):
    ms = jnp.mean(x * x, axis=-1, keepdims=True)
    return x * lax.rsqrt(ms + RMS_EPS) * g


def _dot(a, b):
    return jnp.dot(a, b, preferred_element_type=F32)


def _dot_nt(a, b):
    return lax.dot_general(a, b, (((1,), (1,)), ((), ())), preferred_element_type=F32)


def _mod_kernel(c_ref, w_ref, b_ref, o_ref):
    o_ref[...] = jnp.dot(c_ref[...], w_ref[...], preferred_element_type=F32,
                         precision=lax.Precision.HIGHEST) + b_ref[...]


def _ada_mod(c_all, w, b):
    rows, d = c_all.shape
    n = w.shape[1]
    tn = d
    return pl.pallas_call(
        _mod_kernel,
        out_shape=jax.ShapeDtypeStruct((rows, n), F32),
        grid=(n // tn,),
        in_specs=[pl.BlockSpec((rows, d), lambda i: (0, 0)),
                  pl.BlockSpec((d, tn), lambda i: (0, i)),
                  pl.BlockSpec((1, tn), lambda i: (0, i))],
        out_specs=pl.BlockSpec((rows, tn), lambda i: (0, i)),
        compiler_params=_cparams(1),
        name="ada_mod",
    )(c_all, w, b.reshape(1, n))


def _ffn_kernel(x_ref, mod_ref, g_ref, wgu_ref, wdn_ref, o_ref, *, m0, gi, d_ff):
    x = x_ref[...]
    sh, sc, gt = mod_ref[0, m0], mod_ref[0, m0 + 1], mod_ref[0, m0 + 2]
    h = (_rms(x, g_ref[gi:gi + 1, :]) * (1.0 + sc) + sh).astype(MXU_DTYPE)
    g = _dot(h, wgu_ref[:, :d_ff])
    u = _dot(h, wgu_ref[:, d_ff:])
    a = (g * jax.nn.sigmoid(g) * u).astype(MXU_DTYPE)
    f = _dot(a, wdn_ref[...])
    o_ref[...] = x + 0.5 * gt * _rms(f, g_ref[gi + 1:gi + 2, :])


def _token_tile(t):
    return 512 if t % 512 == 0 else t


def _mod_spec(mod4, n_tiles):
    nb, _, r, d = mod4.shape
    per = n_tiles // nb
    return pl.BlockSpec((1, N_MOD, r, d), lambda i: (i // per, 0, 0, 0))


def _ffn(x, mod4, g, wgu, wdn, m0, gi):
    t, d = x.shape
    d_ff = wdn.shape[0]
    tm = _token_tile(t)
    n_tiles = t // tm
    return pl.pallas_call(
        functools.partial(_ffn_kernel, m0=m0, gi=gi, d_ff=d_ff),
        out_shape=jax.ShapeDtypeStruct((t, d), F32),
        grid=(n_tiles,),
        in_specs=[pl.BlockSpec((tm, d), lambda i: (i, 0)),
                  _mod_spec(mod4, n_tiles),
                  pl.BlockSpec(g.shape, lambda i: (0, 0)),
                  pl.BlockSpec(wgu.shape, lambda i: (0, 0)),
                  pl.BlockSpec(wdn.shape, lambda i: (0, 0))],
        out_specs=pl.BlockSpec((tm, d), lambda i: (i, 0)),
        compiler_params=_cparams(1),
        name="ffn",
    )(x, mod4, g, wgu, wdn)


def _rope(x, cos, sin):
    lane = lax.broadcasted_iota(I32, x.shape, 1)
    first = (lane & (HEAD_DIM - 1)) < (ROPE_DIM // 2)
    partner = jnp.where(first, pltpu.roll(x, LANES - ROPE_DIM // 2, 1), pltpu.roll(x, ROPE_DIM // 2, 1))
    return x * cos + partner * sin


def _proj_kernel(x_ref, mod_ref, g_ref, w_ref, cos_ref, sin_ref,
                 xl_ref, gl_ref, qp_ref, k_ref, v_ref, kb_ref, vb_ref, qip_ref, ki_ref, ki2_ref, wi_ref,
                 *, lw):
    x = x_ref[...]
    h = (_rms(x, g_ref[2:3, :]) * (1.0 + mod_ref[0, 4]) + mod_ref[0, 3]).astype(MXU_DTYPE)
    p = _dot(h, w_ref[...])
    cos, sin = cos_ref[...], sin_ref[...]
    lane = lax.broadcasted_iota(I32, cos.shape, 1)
    lo = lane < HEAD_DIM

    o = 0
    xl_ref[...] = p[:, o:o + lw]
    o += lw
    gl_ref[...] = p[:, o:o + lw]
    o += lw
    q_scale = HEAD_DIM ** -0.5
    for pair in range(N_HEADS // 2):
        qr = _rope(p[:, o + pair * LANES:o + (pair + 1) * LANES], cos, sin) * q_scale
        qr_sw = pltpu.roll(qr, HEAD_DIM, 1)
        for half in range(2):
            hd = 2 * pair + half
            grp = hd // (N_HEADS // N_KV)
            src = qr if half == grp else qr_sw
            slab = jnp.where(lo if grp == 0 else jnp.logical_not(lo), src, 0.0)
            qp_ref[:, hd * LANES:(hd + 1) * LANES] = slab.astype(qp_ref.dtype)
    o += N_HEADS * HEAD_DIM
    k = _rope(p[:, o:o + LANES], cos, sin)
    k_ref[...] = k
    kb_ref[...] = k.astype(kb_ref.dtype)
    o += LANES
    v = p[:, o:o + LANES]
    v_ref[...] = v
    vb_ref[...] = v.astype(vb_ref.dtype)
    o += LANES
    for pair in range(IDX_HEADS // 2):
        qi = _rope(p[:, o + pair * LANES:o + (pair + 1) * LANES], cos, sin)
        for half in range(2):
            hd = 2 * pair + half
            slab = jnp.where(lo if half == 0 else jnp.logical_not(lo), qi, 0.0)
            qip_ref[:, hd * LANES:(hd + 1) * LANES] = slab.astype(qip_ref.dtype)
    o += IDX_HEADS * IDX_DIM
    ki2 = _rope(p[:, o:o + LANES], cos, sin)
    ki_ref[...] = ki2[:, :IDX_DIM]
    ki2_ref[...] = ki2.astype(ki2_ref.dtype)
    o += LANES
    wi_ref[...] = p[:, o:o + LANES] * (IDX_DIM ** -0.5 * IDX_HEADS ** -0.5)


def _proj(x, mod4, g, w_pad, cos, sin, lw, pos_tiles):
    t, d = x.shape
    tm = _token_tile(t)
    n_tiles = t // tm
    tok = lambda i: (i, 0)
    pos = lambda i: (i % pos_tiles, 0)
    widths = [(lw, F32), (lw, F32), (N_HEADS * LANES, MXU_DTYPE), (LANES, F32), (LANES, F32),
              (LANES, MXU_DTYPE), (LANES, MXU_DTYPE), (IDX_HEADS * LANES, MXU_DTYPE),
              (IDX_DIM, F32), (LANES, MXU_DTYPE), (LANES, F32)]
    return pl.pallas_call(
        functools.partial(_proj_kernel, lw=lw),
        out_shape=[jax.ShapeDtypeStruct((t, w), dt) for w, dt in widths],
        grid=(n_tiles,),
        in_specs=[pl.BlockSpec((tm, d), tok),
                  _mod_spec(mod4, n_tiles),
                  pl.BlockSpec(g.shape, lambda i: (0, 0)),
                  pl.BlockSpec(w_pad.shape, lambda i: (0, 0)),
                  pl.BlockSpec((tm, LANES), pos),
                  pl.BlockSpec((tm, LANES), pos)],
        out_specs=[pl.BlockSpec((tm, w), tok) for w, _ in widths],
        compiler_params=_cparams(1),
        name="mixer_proj",
    )(x, mod4, g, w_pad, cos, sin)


def _gelu_tanh(x):
    return x * (0.5 * (1.0 + jnp.tanh(0.7978845608028654 * (x + 0.044715 * (x * x * x)))))


def _lru_coeffs(xc, wa_ref, ba_ref, wx_ref, bx_ref, lam_ref):
    xcb = xc.astype(MXU_DTYPE)
    r = jax.nn.sigmoid(_dot(xcb, wa_ref[...]) + ba_ref[...])
    i = jax.nn.sigmoid(_dot(xcb, wx_ref[...]) + bx_ref[...])
    nl = -lam_ref[...]
    softplus = jnp.maximum(nl, 0.0) + jnp.log1p(jnp.exp(-jnp.abs(nl)))
    log_a = (-LRU_C) * r * softplus
    a = jnp.exp(log_a)
    one_minus_a2 = -jnp.tanh(log_a) * (a * a + 1.0)
    return a, jnp.sqrt(one_minus_a2) * (i * xc)


def _lru_prompt_kernel(xl_ref, gl_ref, buf_ref, h0_ref, cw_ref, cb_ref, wa_ref, ba_ref, wx_ref, bx_ref, lam_ref,
                       y_ref, ht_ref, nb_ref, xs_scr, a_scr, b_scr, h_scr, *, ts):
    tj = pl.program_id(1)
    pad = SUBLANES
    ng = ts // SUBLANES

    @pl.when(tj == 0)
    def _():
        xs_scr[0:pad, :] = jnp.zeros((pad, xs_scr.shape[1]), F32)
        xs_scr[pad - (CONV_W - 1):pad, :] = buf_ref[0]
        h_scr[...] = h0_ref[0]

    x = xl_ref[0]
    xs_scr[pad:pad + ts, :] = x
    xc = cb_ref[...] + x * cw_ref[CONV_W - 1:CONV_W, :]
    for j in range(1, CONV_W):
        xc = xc + xs_scr[pad - j:pad - j + ts, :] * cw_ref[CONV_W - 1 - j:CONV_W - j, :]

    a, b = _lru_coeffs(xc, wa_ref, ba_ref, wx_ref, bx_ref, lam_ref)
    lw = a.shape[1]
    a = a.reshape(ng, SUBLANES, lw)
    b = b.reshape(ng, SUBLANES, lw)
    sub = lax.broadcasted_iota(I32, (ng, SUBLANES, lw), 1)
    d = 1
    while d < SUBLANES:
        keep = sub >= d
        a_sh = jnp.where(keep, pltpu.roll(a, d, 1), 1.0)
        b_sh = jnp.where(keep, pltpu.roll(b, d, 1), 0.0)
        b = a * b_sh + b
        a = a * a_sh
        d *= 2
    a_scr[...] = a
    b_scr[...] = b

    def carry(gidx, h):
        hg = a_scr[gidx] * h + b_scr[gidx]
        b_scr[gidx] = hg
        return hg[SUBLANES - 1:SUBLANES, :]

    h_last = lax.fori_loop(0, ng, carry, h_scr[...], unroll=8)
    h_scr[...] = h_last
    hs = b_scr[...].reshape(ts, lw)
    y_ref[0] = (hs * _gelu_tanh(gl_ref[0])).astype(y_ref.dtype)
    xs_scr[0:pad, :] = xs_scr[ts:ts + pad, :]

    @pl.when(tj == pl.num_programs(1) - 1)
    def _():
        ht_ref[0] = h_last
        nb_ref[0] = xs_scr[pad - (CONV_W - 1):pad, :]


def _lru_prompt(xl, gl, buf0, h0, cw, cb, wa_bd, ba, wx_bd, bx, lam):
    b, s, lw = xl.shape
    ts = 512 if s % 512 == 0 else s
    seq = lambda i, j: (i, j, 0)
    per_b = lambda i, j: (i, 0, 0)
    const = lambda i, j: (0, 0)
    row = pl.BlockSpec((1, lw), const)
    return pl.pallas_call(
        functools.partial(_lru_prompt_kernel, ts=ts),
        out_shape=[jax.ShapeDtypeStruct((b, s, lw), MXU_DTYPE),
                   jax.ShapeDtypeStruct((b, 1, lw), F32),
                   jax.ShapeDtypeStruct((b, CONV_W - 1, lw), F32)],
        grid=(b, s // ts),
        in_specs=[pl.BlockSpec((1, ts, lw), seq), pl.BlockSpec((1, ts, lw), seq),
                  pl.BlockSpec((1, CONV_W - 1, lw), per_b), pl.BlockSpec((1, 1, lw), per_b),
                  pl.BlockSpec((CONV_W, lw), const), row,
                  pl.BlockSpec((lw, lw), const), row, pl.BlockSpec((lw, lw), const), row, row],
        out_specs=[pl.BlockSpec((1, ts, lw), seq), pl.BlockSpec((1, 1, lw), per_b),
                   pl.BlockSpec((1, CONV_W - 1, lw), per_b)],
        scratch_shapes=[pltpu.VMEM((ts + SUBLANES, lw), F32),
                        pltpu.VMEM((ts // SUBLANES, SUBLANES, lw), F32),
                        pltpu.VMEM((ts // SUBLANES, SUBLANES, lw), F32),
                        pltpu.VMEM((1, lw), F32)],
        compiler_params=_cparams(2),
        name="rglru_prompt",
    )(xl, gl, buf0, h0.reshape(b, 1, lw), cw, cb, wa_bd, ba, wx_bd, bx, lam)


def _lru_sample_kernel(xl_ref, gl_ref, buf_ref, h0_ref, cw_ref, cb_ref, wa_ref, ba_ref, wx_ref, bx_ref, lam_ref,
                       y_ref, ht_ref, nb_ref, *, ds):
    rows = [buf_ref[j] for j in range(CONV_W - 1)] + [xl_ref[t] for t in range(ds)]
    xcs = []
    for t in range(ds):
        xc = cb_ref[...]
        for j in range(CONV_W):
            xc = xc + rows[t + j] * cw_ref[j:j + 1, :]
        xcs.append(xc)
    db = xcs[0].shape[0]
    a, b = _lru_coeffs(jnp.concatenate(xcs, axis=0), wa_ref, ba_ref, wx_ref, bx_ref, lam_ref)
    h = h0_ref[...]
    for t in range(ds):
        h = a[t * db:(t + 1) * db] * h + b[t * db:(t + 1) * db]
        y_ref[t] = (h * _gelu_tanh(gl_ref[t])).astype(y_ref.dtype)
    ht_ref[...] = h
    for j in range(CONV_W - 1):
        nb_ref[j] = rows[ds + j]


def _lru_sample(xl_t, gl_t, buf_t, h0, cw, cb, wa_bd, ba, wx_bd, bx, lam):
    ds, db, lw = xl_t.shape
    full = lambda a: pl.BlockSpec(a.shape, lambda i, n=a.ndim: (0,) * n)
    args = (xl_t, gl_t, buf_t, h0, cw, cb, wa_bd, ba, wx_bd, bx, lam)
    outs = [jax.ShapeDtypeStruct((ds, db, lw), MXU_DTYPE), jax.ShapeDtypeStruct((db, lw), F32),
            jax.ShapeDtypeStruct((CONV_W - 1, db, lw), F32)]
    return pl.pallas_call(
        functools.partial(_lru_sample_kernel, ds=ds),
        out_shape=outs,
        grid=(1,),
        in_specs=[full(a) for a in args],
        out_specs=[full(o) for o in outs],
        compiler_params=_cparams(1),
        name="rglru_sample",
    )(*args)


def _score_key(sc):
    bits = lax.bitcast_convert_type(sc, I32)
    key = jnp.where(bits < 0, bits ^ jnp.int32(0x7FFFFFFF), bits)
    return jnp.where(sc == 0.0, 0, key)


def _kth_largest_key(count_ge, rows, n_sel):
    def bit_body(i, t_u):
        cand_u = t_u | jnp.left_shift(jnp.int32(1), 31 - i)
        cnt = count_ge(cand_u ^ jnp.int32(INT_MIN))
        return jnp.where(cnt >= n_sel, cand_u, t_u)

    t_u = lax.fori_loop(0, 32, bit_body, jnp.zeros((rows, 1), I32))
    return t_u ^ jnp.int32(INT_MIN)


def _tie_cutoff(count_tie_below, need, rows, n_bits):
    def bit_body(i, j_u):
        cand = j_u | jnp.left_shift(jnp.int32(1), n_bits - 1 - i)
        return jnp.where(count_tie_below(cand) < need, cand, j_u)

    return lax.fori_loop(0, n_bits, bit_body, jnp.zeros((rows, 1), I32))


def _mask_add(key, col, t, jcut):
    tie = jnp.where(col <= jcut, 0.0, NEG_INF)
    return jnp.where(key > t, 0.0, jnp.where(key == t, tie, NEG_INF))


def _attn_prompt_kernel(qp_ref, qip_ref, wi_ref, ki2_ref, kb_ref, vb_ref, o_ref,
                        key_scr, madd_scr, s_scr, mx_scr, ls_scr, acc_scr, jc_scr, *, n_sel, n_bits):
    jb = pl.program_id(1)
    nch = jb + 1
    qb, kc = Q_BLOCK, Q_BLOCK
    lane = lax.broadcasted_iota(I32, (qb, kc), 1)
    row = lax.broadcasted_iota(I32, (qb, kc), 0)
    qpos = jb * qb + row

    def key_chunk(ref, c):
        return ref[0, pl.ds(pl.multiple_of(c * kc, kc), kc), :]

    qi_all = qip_ref[0]
    qi_stack = jnp.concatenate([qi_all[:, h * LANES:(h + 1) * LANES] for h in range(IDX_HEADS)], axis=0)
    w = wi_ref[0]
    wcols = [w[:, h:h + 1] for h in range(IDX_HEADS)]

    def score_body(c, carry):
        d = _dot_nt(qi_stack, key_chunk(ki2_ref, c))
        sc = jnp.maximum(d[0:qb], 0.0) * wcols[0]
        for h in range(1, IDX_HEADS):
            sc = sc + jnp.maximum(d[h * qb:(h + 1) * qb], 0.0) * wcols[h]
        col = c * kc + lane
        key_scr[c] = jnp.where(col > qpos, jnp.int32(INT_MIN), _score_key(sc))
        return carry

    lax.fori_loop(0, nch, score_body, 0)

    def count(ind):
        def body(c, acc):
            return acc + ind(key_scr[c], c * kc + lane)
        acc = lax.fori_loop(0, nch, body, jnp.zeros((qb, kc), F32))
        return jnp.sum(acc, axis=1, keepdims=True)

    t = _kth_largest_key(lambda cand: count(lambda k, col: jnp.where(k >= cand, 1.0, 0.0)), qb, n_sel)
    cnt_gt = count(lambda k, col: jnp.where(k > t, 1.0, 0.0))
    cnt_ge = count(lambda k, col: jnp.where(k >= t, 1.0, 0.0))
    need = n_sel - cnt_gt
    live = t != jnp.int32(INT_MIN)
    excess = jnp.where(jnp.logical_and(live, cnt_ge > n_sel), 1.0, 0.0)
    jc_scr[...] = jnp.broadcast_to(jnp.where(live, jnp.int32(2 ** 30), jnp.int32(-1)), jc_scr.shape)

    @pl.when(jnp.max(excess) > 0.0)
    def _():
        jcut = _tie_cutoff(
            lambda cand: count(lambda k, col: jnp.where(k == t, jnp.where(col < cand, 1.0, 0.0), 0.0)),
            need, qb, n_bits)
        jc_scr[...] = jnp.broadcast_to(jnp.where(live, jcut, jnp.int32(-1)), jc_scr.shape)

    jcut = jc_scr[:, 0:1]

    def mask_body(c, carry):
        madd_scr[c] = _mask_add(key_scr[c], c * kc + lane, t, jcut)
        return carry

    lax.fori_loop(0, nch, mask_body, 0)

    hpg = N_HEADS // N_KV
    lo = lane < HEAD_DIM
    outs = []
    for g in range(N_KV):
        qg = jnp.concatenate([qp_ref[0, :, (g * hpg + i) * LANES:(g * hpg + i + 1) * LANES] for i in range(hpg)],
                             axis=0)
        mx_scr[...] = jnp.full(mx_scr.shape, NEG_INF, F32)

        def pass1(c, carry):
            s = _dot_nt(qg, key_chunk(kb_ref, c))
            m_add = madd_scr[c]
            for i in range(hpg):
                si = s[i * qb:(i + 1) * qb] + m_add
                s_scr[c, i * qb:(i + 1) * qb, :] = si
                mx_scr[i * qb:(i + 1) * qb, :] = jnp.maximum(mx_scr[i * qb:(i + 1) * qb, :], si)
            return carry

        lax.fori_loop(0, nch, pass1, 0)
        m = jnp.max(mx_scr[...], axis=1, keepdims=True)
        ls_scr[...] = jnp.zeros(ls_scr.shape, F32)
        acc_scr[...] = jnp.zeros(acc_scr.shape, F32)

        def pass2(c, carry):
            p = jnp.exp(s_scr[c] - m)
            ls_scr[...] += p
            acc_scr[...] += _dot(p.astype(MXU_DTYPE), key_chunk(vb_ref, c))
            return carry

        lax.fori_loop(0, nch, pass2, 0)
        o = acc_scr[...] / jnp.sum(ls_scr[...], axis=1, keepdims=True)
        for pair in range(hpg // 2):
            even = o[(2 * pair) * qb:(2 * pair + 1) * qb]
            odd = o[(2 * pair + 1) * qb:(2 * pair + 2) * qb]
            if g == 0:
                outs.append(jnp.where(lo, even, pltpu.roll(odd, HEAD_DIM, 1)))
            else:
                outs.append(jnp.where(lo, pltpu.roll(even, HEAD_DIM, 1), odd))
    for pair, val in enumerate(outs):
        o_ref[0, :, pair * LANES:(pair + 1) * LANES] = val.astype(o_ref.dtype)


def _attn_prompt(qp, qip, wi, ki2, kb, vb, n_sel):
    b, s, _ = qp.shape
    nq = s // Q_BLOCK
    n_bits = max(1, (s - 1).bit_length())
    blk = lambda i, j: (i, j, 0)
    seq = lambda i, j: (i, 0, 0)
    hq = (N_HEADS // N_KV) * Q_BLOCK
    return pl.pallas_call(
        functools.partial(_attn_prompt_kernel, n_sel=n_sel, n_bits=n_bits),
        out_shape=jax.ShapeDtypeStruct((b, s, N_HEADS * HEAD_DIM), MXU_DTYPE),
        grid=(b, nq),
        in_specs=[pl.BlockSpec((1, Q_BLOCK, qp.shape[2]), blk),
                  pl.BlockSpec((1, Q_BLOCK, qip.shape[2]), blk),
                  pl.BlockSpec((1, Q_BLOCK, LANES), blk),
                  pl.BlockSpec((1, s, LANES), seq),
                  pl.BlockSpec((1, s, LANES), seq),
                  pl.BlockSpec((1, s, LANES), seq)],
        out_specs=pl.BlockSpec((1, Q_BLOCK, N_HEADS * HEAD_DIM), blk),
        scratch_shapes=[pltpu.VMEM((nq, Q_BLOCK, Q_BLOCK), I32),
                        pltpu.VMEM((nq, Q_BLOCK, Q_BLOCK), F32),
                        pltpu.VMEM((nq, hq, Q_BLOCK), F32),
                        pltpu.VMEM((hq, Q_BLOCK), F32),
                        pltpu.VMEM((hq, Q_BLOCK), F32),
                        pltpu.VMEM((hq, LANES), F32),
                        pltpu.VMEM((Q_BLOCK, LANES), I32)],
        compiler_params=_cparams(2),
        name="attn_prompt",
    )(qp, qip, wi, ki2, kb, vb)


def _out_kernel(x_ref, yl_ref, ya_ref, mod_ref, g_ref, wo_ref, o_ref, *, lw):
    y = _dot(yl_ref[...], wo_ref[:lw, :]) + _dot(ya_ref[...], wo_ref[lw:, :])
    o_ref[...] = x_ref[...] + mod_ref[0, 5] * _rms(y, g_ref[3:4, :])


def _out_proj(x, y_lru, y_att, mod4, g, wo):
    t, d = x.shape
    lw = y_lru.shape[1]
    tm = _token_tile(t)
    n_tiles = t // tm
    tok = lambda i: (i, 0)
    return pl.pallas_call(
        functools.partial(_out_kernel, lw=lw),
        out_shape=jax.ShapeDtypeStruct((t, d), F32),
        grid=(n_tiles,),
        in_specs=[pl.BlockSpec((tm, d), tok), pl.BlockSpec((tm, lw), tok),
                  pl.BlockSpec((tm, y_att.shape[1]), tok), _mod_spec(mod4, n_tiles),
                  pl.BlockSpec(g.shape, lambda i: (0, 0)), pl.BlockSpec(wo.shape, lambda i: (0, 0))],
        out_specs=pl.BlockSpec((tm, d), tok),
        compiler_params=_cparams(1),
        name="out_proj",
    )(x, y_lru, y_att, mod4, g, wo)


def _attn_sample_kernel(pt_ref, qs_ref, qis_ref, wcol_ref, kin_ref, kn_ref, vn_ref, ckidx_ref, ck_ref, cv_ref,
                        o_ref, kidx_buf, k_buf, v_buf, key_scr, madd_scr, s_scr, sems,
                        *, layer, n_pages, page, ds, n_sel, n_bits, kc):
    bi = pl.program_id(0)
    past = n_pages * page
    total = past + LANES
    n_chunks = past // kc

    streams = ((ckidx_ref, kidx_buf), (ck_ref, k_buf), (cv_ref, v_buf))

    def page_copy(which, p):
        src, dst = streams[which]
        rows = pl.ds(pl.multiple_of(p * page, page), page)
        return pltpu.make_async_copy(src.at[layer, pt_ref[bi, p]], dst.at[rows], sems.at[which])

    def start_all(p, carry):
        for which in range(len(streams)):
            page_copy(which, p).start()
        return carry

    lax.fori_loop(0, n_pages, start_all, 0)

    def wait_all(which):
        def body(p, carry):
            page_copy(which, p).wait()
            return carry
        lax.fori_loop(0, n_pages, body, 0)

    zeros_tail = jnp.zeros((LANES, LANES), F32)
    kidx_buf[past:total, :] = zeros_tail[:, :IDX_DIM]
    k_buf[past:total, :] = zeros_tail
    v_buf[past:total, :] = zeros_tail
    kidx_buf[past:past + ds, :] = kin_ref[0]
    k_buf[past:past + ds, :] = kn_ref[0]
    v_buf[past:past + ds, :] = vn_ref[0]
    key_scr[...] = jnp.full(key_scr.shape, INT_MIN, I32)

    qis = qis_ref[0]
    wcol = wcol_ref[0][:, 0:1]
    wait_all(0)

    def scores(lo_col, width):
        d = _dot_nt(qis, kidx_buf[lo_col:lo_col + width, :].astype(MXU_DTYPE))
        wgt = jnp.maximum(d, 0.0) * wcol
        sc = wgt[0:ds]
        for h in range(1, IDX_HEADS):
            sc = sc + wgt[h * ds:(h + 1) * ds]
        return sc

    for c in range(n_chunks):
        key_scr[0:ds, c * kc:(c + 1) * kc] = _score_key(scores(c * kc, kc))
    sc_new = scores(past, LANES)
    lane_new = lax.broadcasted_iota(I32, (ds, LANES), 1)
    row_new = lax.broadcasted_iota(I32, (ds, LANES), 0)
    key_scr[0:ds, past:total] = jnp.where(lane_new > row_new, jnp.int32(INT_MIN), _score_key(sc_new))

    rows = key_scr.shape[0]
    col = lax.broadcasted_iota(I32, (rows, total), 1)

    def count(ind):
        return jnp.sum(ind(key_scr[...]), axis=1, keepdims=True)

    t = _kth_largest_key(lambda cand: count(lambda k: jnp.where(k >= cand, 1.0, 0.0)), rows, n_sel)
    cnt_gt = count(lambda k: jnp.where(k > t, 1.0, 0.0))
    need = n_sel - cnt_gt
    live = t != jnp.int32(INT_MIN)
    jcut = _tie_cutoff(lambda cand: count(lambda k: jnp.where(k == t, jnp.where(col < cand, 1.0, 0.0), 0.0)),
                       need, rows, n_bits)
    jcut = jnp.where(live, jcut, jnp.int32(-1))
    madd_scr[...] = _mask_add(key_scr[...], col, t, jcut)

    qs = qs_ref[0]
    nrow = qs.shape[0]

    def row_mask(lo_col, width):
        return jnp.concatenate(
            [jnp.broadcast_to(madd_scr[tt:tt + 1, lo_col:lo_col + width], (N_HEADS, width)) for tt in range(ds)],
            axis=0)

    spans = [(c * kc, kc) for c in range(n_chunks)] + [(past, LANES)]
    wait_all(1)
    m = jnp.full((nrow, 1), NEG_INF, F32)
    for lo_col, width in spans:
        s = _dot_nt(qs, k_buf[lo_col:lo_col + width, :].astype(MXU_DTYPE)) + row_mask(lo_col, width)
        s_scr[:, lo_col:lo_col + width] = s
        m = jnp.maximum(m, jnp.max(s, axis=1, keepdims=True))
    wait_all(2)
    l = jnp.zeros((nrow, 1), F32)
    acc = jnp.zeros((nrow, LANES), F32)
    for lo_col, width in spans:
        p = jnp.exp(s_scr[:, lo_col:lo_col + width] - m)
        l = l + jnp.sum(p, axis=1, keepdims=True)
        acc = acc + _dot(p.astype(MXU_DTYPE), v_buf[lo_col:lo_col + width, :].astype(MXU_DTYPE))
    o_ref[0] = acc / l


def _attn_sample(page_table, qs, qis, wcol, ki_new, k_new, v_new, cache_kidx, cache_k2, cache_v2, layer, n_sel):
    db, n_pages = page_table.shape
    page = cache_kidx.shape[2]
    ds = ki_new.shape[1]
    past = n_pages * page
    total = past + LANES
    kc = 2048 if past % 2048 == 0 else page
    n_bits = max(1, (total - 1).bit_length())
    per = lambda i, pt: (i, 0, 0)
    any_spec = pl.BlockSpec(memory_space=pl.ANY)
    grid_spec = pltpu.PrefetchScalarGridSpec(
        num_scalar_prefetch=1,
        grid=(db,),
        in_specs=[pl.BlockSpec((1,) + qs.shape[1:], per), pl.BlockSpec((1,) + qis.shape[1:], per),
                  pl.BlockSpec((1,) + wcol.shape[1:], per), pl.BlockSpec((1,) + ki_new.shape[1:], per),
                  pl.BlockSpec((1,) + k_new.shape[1:], per), pl.BlockSpec((1,) + v_new.shape[1:], per),
                  any_spec, any_spec, any_spec],
        out_specs=pl.BlockSpec((1, qs.shape[1], LANES), per),
        scratch_shapes=[pltpu.VMEM((total, IDX_DIM), F32),
                        pltpu.VMEM((total, LANES), F32),
                        pltpu.VMEM((total, LANES), F32),
                        pltpu.VMEM((SUBLANES, total), I32),
                        pltpu.VMEM((SUBLANES, total), F32),
                        pltpu.VMEM((qs.shape[1], total), F32),
                        pltpu.SemaphoreType.DMA((3,))])
    return pl.pallas_call(
        functools.partial(_attn_sample_kernel, layer=layer, n_pages=n_pages, page=page, ds=ds, n_sel=n_sel,
                          n_bits=n_bits, kc=kc),
        out_shape=jax.ShapeDtypeStruct((db, qs.shape[1], LANES), F32),
        grid_spec=grid_spec,
        compiler_params=_cparams(1),
        name="attn_sample",
    )(page_table, qs, qis, wcol, ki_new, k_new, v_new, cache_kidx, cache_k2, cache_v2)


def _rope_tables(pos):
    half = ROPE_DIM // 2
    freqs = ROPE_THETA ** (-(jnp.arange(half, dtype=F32) / half))
    ang = pos.astype(F32)[:, None] * freqs[None, :]
    cos, sin = jnp.cos(ang), jnp.sin(ang)
    rest = HEAD_DIM - ROPE_DIM
    c64 = jnp.concatenate([cos, cos, jnp.ones((pos.shape[0], rest), F32)], axis=1)
    s64 = jnp.concatenate([-sin, sin, jnp.zeros((pos.shape[0], rest), F32)], axis=1)
    return jnp.tile(c64, (1, LANES // HEAD_DIM)), jnp.tile(s64, (1, LANES // HEAD_DIM))


def _block_diag(w):
    nb, n, _ = w.shape
    eye = jnp.eye(nb, dtype=w.dtype)
    return (eye[:, None, :, None] * w[:, :, None, :]).reshape(nb * n, nb * n)


def kernel(x_prompt, x_sample, c_prompt, c_sample, cache_k, cache_v, cache_kidx, state_h, state_conv, page_table,
           w_ada, b_ada, norm_g, ffn_w_gu, ffn_w_down, w_in, conv_w, conv_b, lru_wa, lru_ba, lru_wx, lru_bx,
           lru_lambda, w_out):
    b, s, d = x_prompt.shape
    db, ds, _ = x_sample.shape
    depth = w_ada.shape[0]
    lw = lru_lambda.shape[1]
    n_pool, page = cache_k.shape[1], cache_k.shape[2]
    past = page_table.shape[1] * page
    kv_w = N_KV * HEAD_DIM
    assert kv_w == LANES and 2 * IDX_DIM == LANES and s % Q_BLOCK == 0 and ds <= SUBLANES

    cos_p, sin_p = _rope_tables(jnp.arange(s, dtype=I32))
    cos_s, sin_s = _rope_tables(past + jnp.arange(ds, dtype=I32))
    cos_s, sin_s = jnp.tile(cos_s, (db, 1)), jnp.tile(sin_s, (db, 1))

    n_c = b + db
    c_all = jnp.concatenate([c_prompt, c_sample, jnp.zeros((-n_c % SUBLANES, d), F32)], axis=0)
    cache_k2 = cache_k.reshape(depth, n_pool, page, kv_w)
    cache_v2 = cache_v.reshape(depth, n_pool, page, kv_w)

    xp = x_prompt.reshape(b * s, d)
    xs = x_sample.reshape(db * ds, d)
    n_sel_p = min(TOP_K, s // 4)
    n_sel_s = min(TOP_K, (past + ds) // 4)
    q_w = N_HEADS * HEAD_DIM
    qi_w = IDX_HEADS * IDX_DIM
    col_ki = 2 * lw + q_w + 2 * kv_w + qi_w

    outs = {name: [] for name in ("kp", "vp", "kip", "hp", "cp", "ks", "vs", "kis", "hs", "cs")}
    for l in range(depth):
        mod = _ada_mod(c_all, w_ada[l], b_ada[l])
        mod_p = mod[:b].reshape(b, N_MOD, 1, d)
        mod_s = jnp.repeat(mod[b:n_c].reshape(db, N_MOD, d), ds, axis=0).transpose(1, 0, 2)[None]
        g = norm_g[l]
        wgu = ffn_w_gu[l].astype(MXU_DTYPE)
        wdn = ffn_w_down[l].astype(MXU_DTYPE)
        w_ki = w_in[l][:, col_ki:col_ki + IDX_DIM]
        w_pad = jnp.concatenate(
            [w_in[l][:, :col_ki], w_ki, w_ki, w_in[l][:, col_ki + IDX_DIM:],
             jnp.zeros((d, LANES - IDX_HEADS), F32)], axis=1).astype(MXU_DTYPE)
        wo = w_out[l].astype(MXU_DTYPE)
        wa_bd = _block_diag(lru_wa[l]).astype(MXU_DTYPE)
        wx_bd = _block_diag(lru_wx[l]).astype(MXU_DTYPE)
        lru_args = (conv_w[l], conv_b[l].reshape(1, lw), wa_bd, lru_ba[l].reshape(1, lw), wx_bd,
                    lru_bx[l].reshape(1, lw), lru_lambda[l].reshape(1, lw))

        xp = _ffn(xp, mod_p, g, wgu[0], wdn[0], 0, 0)
        xl, gl, qp, k, v, kb, vb, qip, ki, ki2, wi = _proj(xp, mod_p, g, w_pad, cos_p, sin_p, lw, s // _token_tile(b * s))
        r3 = lambda a: a.reshape(b, s, a.shape[-1])
        y_lru, h_t, nbuf = _lru_prompt(r3(xl), r3(gl), jnp.zeros((b, CONV_W - 1, lw), F32), jnp.zeros((b, lw), F32),
                                       *lru_args)
        y_att = _attn_prompt(r3(qp), r3(qip), r3(wi), r3(ki2), r3(kb), r3(vb), n_sel_p)
        xp = _out_proj(xp, y_lru.reshape(b * s, lw), y_att.reshape(b * s, q_w), mod_p, g, wo)
        xp = _ffn(xp, mod_p, g, wgu[1], wdn[1], 6, 4)
        outs["kp"].append(k.reshape(b, s, N_KV, HEAD_DIM))
        outs["vp"].append(v.reshape(b, s, N_KV, HEAD_DIM))
        outs["kip"].append(ki.reshape(b, s, IDX_DIM))
        outs["hp"].append(h_t.reshape(b, lw))
        outs["cp"].append(nbuf)

        xs = _ffn(xs, mod_s, g, wgu[0], wdn[0], 0, 0)
        xl, gl, qp, k, v, kb, vb, qip, ki, ki2, wi = _proj(xs, mod_s, g, w_pad, cos_s, sin_s, lw, 1)
        tmaj = lambda a: a.reshape(db, ds, a.shape[-1]).transpose(1, 0, 2)
        y_lru, h_t, nbuf = _lru_sample(tmaj(xl), tmaj(gl), state_conv[l].transpose(1, 0, 2), state_h[l], *lru_args)
        qs = qp.reshape(db, ds * N_HEADS, LANES)
        qis = qip.reshape(db, ds, IDX_HEADS, 2, IDX_DIM)
        qis = jnp.stack([qis[:, :, h, h % 2] for h in range(IDX_HEADS)], axis=1).reshape(db, IDX_HEADS * ds, IDX_DIM)
        wcol = wi.reshape(db, ds, LANES)[:, :, :IDX_HEADS].transpose(0, 2, 1).reshape(db, IDX_HEADS * ds, 1)
        wcol = jnp.broadcast_to(wcol, (db, IDX_HEADS * ds, LANES))
        o_s = _attn_sample(page_table, qs, qis, wcol, ki.reshape(db, ds, IDX_DIM), k.reshape(db, ds, kv_w),
                           v.reshape(db, ds, kv_w), cache_kidx, cache_k2, cache_v2, l, n_sel_s)
        o_s = o_s.reshape(db, ds, N_HEADS, N_KV, HEAD_DIM)
        hpg = N_HEADS // N_KV
        y_att = jnp.stack([o_s[:, :, h, h // hpg] for h in range(N_HEADS)], axis=2).reshape(db * ds, q_w)
        xs = _out_proj(xs, y_lru.transpose(1, 0, 2).reshape(db * ds, lw), y_att.astype(MXU_DTYPE), mod_s, g, wo)
        xs = _ffn(xs, mod_s, g, wgu[1], wdn[1], 6, 4)
        outs["ks"].append(k.reshape(db, ds, N_KV, HEAD_DIM))
        outs["vs"].append(v.reshape(db, ds, N_KV, HEAD_DIM))
        outs["kis"].append(ki.reshape(db, ds, IDX_DIM))
        outs["hs"].append(h_t)
        outs["cs"].append(nbuf.transpose(1, 0, 2))

    st = lambda name: jnp.stack(outs[name])
    return (xp.reshape(b, s, d), xs.reshape(db, ds, d), st("kp"), st("vp"), st("kip"), st("hp"), st("cp"),
            st("ks"), st("vs"), st("kis"), st("hs"), st("cs"))
```

```python
import functools

import jax
import jax.numpy as jnp
from jax import lax
from jax.experimental import pallas as pl
from jax.experimental.pallas import tpu as pltpu

F32 = jnp.float32
I32 = jnp.int32
MXU_DTYPE = jnp.bfloat16

LRU_BLOCKS = 8
CONV_W = 4
LRU_C = 8.0
N_HEADS = 8
HEAD_DIM = 64
N_KV = 2
ROPE_DIM = HEAD_DIM // 4
ROPE_THETA = 500000.0
IDX_HEADS = 4
IDX_DIM = 64
TOP_K = 256
Q_BLOCK = 128
N_MOD = 9
ATT_QUERY_ROWS = 256
ATT_KEY_CHUNK = 512
RMS_EPS = 1e-6

LANES = 128
SUBLANES = 8
VMEM_LIMIT = 56 * 1024 * 1024
INT_MIN = -2147483648
NEG_INF = float("-inf")


def _cparams(n_axes):
    return pltpu.CompilerParams(dimension_semantics=("arbitrary",) * n_axes,
                                vmem_limit_bytes=VMEM_LIMIT)


def _rms(x, g):
    ms = jnp.mean(x * x, axis=-1, keepdims=True)
    return x * lax.rsqrt(ms + RMS_EPS) * g


def _dot(a, b):
    return jnp.dot(a, b, preferred_element_type=F32)


def _dot_nt(a, b):
    return lax.dot_general(a, b, (((1,), (1,)), ((), ())), preferred_element_type=F32)


def _mod_kernel(c_ref, w_ref, b_ref, o_ref):
    o_ref[...] = jnp.dot(c_ref[...], w_ref[...], preferred_element_type=F32,
                         precision=lax.Precision.HIGHEST) + b_ref[...]


def _ada_mod(c_all, w, b):
    rows, d = c_all.shape
    n = w.shape[1]
    tn = d
    return pl.pallas_call(
        _mod_kernel,
        out_shape=jax.ShapeDtypeStruct((rows, n), F32),
        grid=(n // tn,),
        in_specs=[pl.BlockSpec((rows, d), lambda i: (0, 0)),
                  pl.BlockSpec((d, tn), lambda i: (0, i)),
                  pl.BlockSpec((1, tn), lambda i: (0, i))],
        out_specs=pl.BlockSpec((rows, tn), lambda i: (0, i)),
        compiler_params=_cparams(1),
        name="ada_mod",
    )(c_all, w, b.reshape(1, n))


def _ffn_kernel(x_ref, mod_ref, g_ref, wgu_ref, wdn_ref, o_ref, *, m0, gi, d_ff):
    x = x_ref[...]
    sh, sc, gt = mod_ref[0, m0], mod_ref[0, m0 + 1], mod_ref[0, m0 + 2]
    h = (_rms(x, g_ref[gi:gi + 1, :]) * (1.0 + sc) + sh).astype(MXU_DTYPE)
    g = _dot(h, wgu_ref[:, :d_ff])
    u = _dot(h, wgu_ref[:, d_ff:])
    a = (g * jax.nn.sigmoid(g) * u).astype(MXU_DTYPE)
    f = _dot(a, wdn_ref[...])
    o_ref[...] = x + 0.5 * gt * _rms(f, g_ref[gi + 1:gi + 2, :])


def _token_tile(t):
    return 512 if t % 512 == 0 else t


def _mod_spec(mod4, n_tiles):
    nb, _, r, d = mod4.shape
    per = n_tiles // nb
    return pl.BlockSpec((1, N_MOD, r, d), lambda i: (i // per, 0, 0, 0))


def _ffn(x, mod4, g, wgu, wdn, m0, gi):
    t, d = x.shape
    d_ff = wdn.shape[0]
    tm = _token_tile(t)
    n_tiles = t // tm
    return pl.pallas_call(
        functools.partial(_ffn_kernel, m0=m0, gi=gi, d_ff=d_ff),
        out_shape=jax.ShapeDtypeStruct((t, d), F32),
        grid=(n_tiles,),
        in_specs=[pl.BlockSpec((tm, d), lambda i: (i, 0)),
                  _mod_spec(mod4, n_tiles),
                  pl.BlockSpec(g.shape, lambda i: (0, 0)),
                  pl.BlockSpec(wgu.shape, lambda i: (0, 0)),
                  pl.BlockSpec(wdn.shape, lambda i: (0, 0))],
        out_specs=pl.BlockSpec((tm, d), lambda i: (i, 0)),
        compiler_params=_cparams(1),
        name="ffn",
    )(x, mod4, g, wgu, wdn)


def _rope(x, cos, sin):
    lane = lax.broadcasted_iota(I32, x.shape, 1)
    first = (lane & (HEAD_DIM - 1)) < (ROPE_DIM // 2)
    partner = jnp.where(first, pltpu.roll(x, LANES - ROPE_DIM // 2, 1), pltpu.roll(x, ROPE_DIM // 2, 1))
    return x * cos + partner * sin


def _proj_kernel(x_ref, mod_ref, g_ref, w_ref, cos_ref, sin_ref,
                 xl_ref, gl_ref, qp_ref, k_ref, v_ref, kb_ref, vb_ref, qip_ref, ki_ref, ki2_ref, wi_ref,
                 *, lw):
    x = x_ref[...]
    h = (_rms(x, g_ref[2:3, :]) * (1.0 + mod_ref[0, 4]) + mod_ref[0, 3]).astype(MXU_DTYPE)
    p = _dot(h, w_ref[...])
    cos, sin = cos_ref[...], sin_ref[...]
    lane = lax.broadcasted_iota(I32, cos.shape, 1)
    lo = lane < HEAD_DIM

    o = 0
    xl_ref[...] = p[:, o:o + lw]
    o += lw
    gl_ref[...] = p[:, o:o + lw]
    o += lw
    q_scale = HEAD_DIM ** -0.5
    for pair in range(N_HEADS // 2):
        qr = _rope(p[:, o + pair * LANES:o + (pair + 1) * LANES], cos, sin) * q_scale
        qr_sw = pltpu.roll(qr, HEAD_DIM, 1)
        for half in range(2):
            hd = 2 * pair + half
            grp = hd // (N_HEADS // N_KV)
            src = qr if half == grp else qr_sw
            slab = jnp.where(lo if grp == 0 else jnp.logical_not(lo), src, 0.0)
            qp_ref[:, hd * LANES:(hd + 1) * LANES] = slab.astype(qp_ref.dtype)
    o += N_HEADS * HEAD_DIM
    k = _rope(p[:, o:o + LANES], cos, sin)
    k_ref[...] = k
    kb_ref[...] = k.astype(kb_ref.dtype)
    o += LANES
    v = p[:, o:o + LANES]
    v_ref[...] = v
    vb_ref[...] = v.astype(vb_ref.dtype)
    o += LANES
    for pair in range(IDX_HEADS // 2):
        qi = _rope(p[:, o + pair * LANES:o + (pair + 1) * LANES], cos, sin)
        for half in range(2):
            hd = 2 * pair + half
            slab = jnp.where(lo if half == 0 else jnp.logical_not(lo), qi, 0.0)
            qip_ref[:, hd * LANES:(hd + 1) * LANES] = slab.astype(qip_ref.dtype)
    o += IDX_HEADS * IDX_DIM
    ki2 = _rope(p[:, o:o + LANES], cos, sin)
    ki_ref[...] = ki2[:, :IDX_DIM]
    ki2_ref[...] = ki2.astype(ki2_ref.dtype)
    o += LANES
    wi_ref[...] = p[:, o:o + LANES] * (IDX_DIM ** -0.5 * IDX_HEADS ** -0.5)


def _proj(x, mod4, g, w_pad, cos, sin, lw, pos_tiles):
    t, d = x.shape
    tm = _token_tile(t)
    n_tiles = t // tm
    tok = lambda i: (i, 0)
    pos = lambda i: (i % pos_tiles, 0)
    widths = [(lw, F32), (lw, F32), (N_HEADS * LANES, MXU_DTYPE), (LANES, F32), (LANES, F32),
              (LANES, MXU_DTYPE), (LANES, MXU_DTYPE), (IDX_HEADS * LANES, MXU_DTYPE),
              (IDX_DIM, F32), (LANES, MXU_DTYPE), (LANES, F32)]
    return pl.pallas_call(
        functools.partial(_proj_kernel, lw=lw),
        out_shape=[jax.ShapeDtypeStruct((t, w), dt) for w, dt in widths],
        grid=(n_tiles,),
        in_specs=[pl.BlockSpec((tm, d), tok),
                  _mod_spec(mod4, n_tiles),
                  pl.BlockSpec(g.shape, lambda i: (0, 0)),
                  pl.BlockSpec(w_pad.shape, lambda i: (0, 0)),
                  pl.BlockSpec((tm, LANES), pos),
                  pl.BlockSpec((tm, LANES), pos)],
        out_specs=[pl.BlockSpec((tm, w), tok) for w, _ in widths],
        compiler_params=_cparams(1),
        name="mixer_proj",
    )(x, mod4, g, w_pad, cos, sin)


def _gelu_tanh(x):
    return x * (0.5 * (1.0 + jnp.tanh(0.7978845608028654 * (x + 0.044715 * (x * x * x)))))


def _lru_coeffs(xc, wa_ref, ba_ref, wx_ref, bx_ref, lam_ref):
    xcb = xc.astype(MXU_DTYPE)
    r = jax.nn.sigmoid(_dot(xcb, wa_ref[...]) + ba_ref[...])
    i = jax.nn.sigmoid(_dot(xcb, wx_ref[...]) + bx_ref[...])
    nl = -lam_ref[...]
    softplus = jnp.maximum(nl, 0.0) + jnp.log1p(jnp.exp(-jnp.abs(nl)))
    log_a = (-LRU_C) * r * softplus
    a = jnp.exp(log_a)
    one_minus_a2 = -jnp.tanh(log_a) * (a * a + 1.0)
    return a, jnp.sqrt(one_minus_a2) * (i * xc)


def _lru_prompt_kernel(xl_ref, gl_ref, buf_ref, h0_ref, cw_ref, cb_ref, wa_ref, ba_ref, wx_ref, bx_ref, lam_ref,
                       y_ref, ht_ref, nb_ref, xs_scr, a_scr, b_scr, h_scr, *, ts):
    tj = pl.program_id(1)
    pad = SUBLANES
    ng = ts // SUBLANES

    @pl.when(tj == 0)
    def _():
        xs_scr[0:pad, :] = jnp.zeros((pad, xs_scr.shape[1]), F32)
        xs_scr[pad - (CONV_W - 1):pad, :] = buf_ref[0]
        h_scr[...] = h0_ref[0]

    x = xl_ref[0]
    xs_scr[pad:pad + ts, :] = x
    xc = cb_ref[...] + x * cw_ref[CONV_W - 1:CONV_W, :]
    for j in range(1, CONV_W):
        xc = xc + xs_scr[pad - j:pad - j + ts, :] * cw_ref[CONV_W - 1 - j:CONV_W - j, :]

    a, b = _lru_coeffs(xc, wa_ref, ba_ref, wx_ref, bx_ref, lam_ref)
    lw = a.shape[1]
    a = a.reshape(ng, SUBLANES, lw)
    b = b.reshape(ng, SUBLANES, lw)
    sub = lax.broadcasted_iota(I32, (ng, SUBLANES, lw), 1)
    d = 1
    while d < SUBLANES:
        keep = sub >= d
        a_sh = jnp.where(keep, pltpu.roll(a, d, 1), 1.0)
        b_sh = jnp.where(keep, pltpu.roll(b, d, 1), 0.0)
        b = a * b_sh + b
        a = a * a_sh
        d *= 2
    a_scr[...] = a
    b_scr[...] = b

    def carry(gidx, h):
        hg = a_scr[gidx] * h + b_scr[gidx]
        b_scr[gidx] = hg
        return hg[SUBLANES - 1:SUBLANES, :]

    h_last = lax.fori_loop(0, ng, carry, h_scr[...], unroll=8)
    h_scr[...] = h_last
    hs = b_scr[...].reshape(ts, lw)
    y_ref[0] = (hs * _gelu_tanh(gl_ref[0])).astype(y_ref.dtype)
    xs_scr[0:pad, :] = xs_scr[ts:ts + pad, :]

    @pl.when(tj == pl.num_programs(1) - 1)
    def _():
        ht_ref[0] = h_last
        nb_ref[0] = xs_scr[pad - (CONV_W - 1):pad, :]


def _lru_prompt(xl, gl, buf0, h0, cw, cb, wa_bd, ba, wx_bd, bx, lam):
    b, s, lw = xl.shape
    ts = 512 if s % 512 == 0 else s
    seq = lambda i, j: (i, j, 0)
    per_b = lambda i, j: (i, 0, 0)
    const = lambda i, j: (0, 0)
    row = pl.BlockSpec((1, lw), const)
    return pl.pallas_call(
        functools.partial(_lru_prompt_kernel, ts=ts),
        out_shape=[jax.ShapeDtypeStruct((b, s, lw), MXU_DTYPE),
                   jax.ShapeDtypeStruct((b, 1, lw), F32),
                   jax.ShapeDtypeStruct((b, CONV_W - 1, lw), F32)],
        grid=(b, s // ts),
        in_specs=[pl.BlockSpec((1, ts, lw), seq), pl.BlockSpec((1, ts, lw), seq),
                  pl.BlockSpec((1, CONV_W - 1, lw), per_b), pl.BlockSpec((1, 1, lw), per_b),
                  pl.BlockSpec((CONV_W, lw), const), row,
                  pl.BlockSpec((lw, lw), const), row, pl.BlockSpec((lw, lw), const), row, row],
        out_specs=[pl.BlockSpec((1, ts, lw), seq), pl.BlockSpec((1, 1, lw), per_b),
                   pl.BlockSpec((1, CONV_W - 1, lw), per_b)],
        scratch_shapes=[pltpu.VMEM((ts + SUBLANES, lw), F32),
                        pltpu.VMEM((ts // SUBLANES, SUBLANES, lw), F32),
                        pltpu.VMEM((ts // SUBLANES, SUBLANES, lw), F32),
                        pltpu.VMEM((1, lw), F32)],
        compiler_params=_cparams(2),
        name="rglru_prompt",
    )(xl, gl, buf0, h0.reshape(b, 1, lw), cw, cb, wa_bd, ba, wx_bd, bx, lam)


def _lru_sample_kernel(xl_ref, gl_ref, buf_ref, h0_ref, cw_ref, cb_ref, wa_ref, ba_ref, wx_ref, bx_ref, lam_ref,
                       y_ref, ht_ref, nb_ref, *, ds):
    rows = [buf_ref[j] for j in range(CONV_W - 1)] + [xl_ref[t] for t in range(ds)]
    xcs = []
    for t in range(ds):
        xc = cb_ref[...]
        for j in range(CONV_W):
            xc = xc + rows[t + j] * cw_ref[j:j + 1, :]
        xcs.append(xc)
    db = xcs[0].shape[0]
    a, b = _lru_coeffs(jnp.concatenate(xcs, axis=0), wa_ref, ba_ref, wx_ref, bx_ref, lam_ref)
    h = h0_ref[...]
    for t in range(ds):
        h = a[t * db:(t + 1) * db] * h + b[t * db:(t + 1) * db]
        y_ref[t] = (h * _gelu_tanh(gl_ref[t])).astype(y_ref.dtype)
    ht_ref[...] = h
    for j in range(CONV_W - 1):
        nb_ref[j] = rows[ds + j]


def _lru_sample(xl_t, gl_t, buf_t, h0, cw, cb, wa_bd, ba, wx_bd, bx, lam):
    ds, db, lw = xl_t.shape
    full = lambda a: pl.BlockSpec(a.shape, lambda i, n=a.ndim: (0,) * n)
    args = (xl_t, gl_t, buf_t, h0, cw, cb, wa_bd, ba, wx_bd, bx, lam)
    outs = [jax.ShapeDtypeStruct((ds, db, lw), MXU_DTYPE), jax.ShapeDtypeStruct((db, lw), F32),
            jax.ShapeDtypeStruct((CONV_W - 1, db, lw), F32)]
    return pl.pallas_call(
        functools.partial(_lru_sample_kernel, ds=ds),
        out_shape=outs,
        grid=(1,),
        in_specs=[full(a) for a in args],
        out_specs=[full(o) for o in outs],
        compiler_params=_cparams(1),
        name="rglru_sample",
    )(*args)


def _score_key(sc):
    bits = lax.bitcast_convert_type(sc, I32)
    key = jnp.where(bits < 0, bits ^ jnp.int32(0x7FFFFFFF), bits)
    return jnp.where(sc == 0.0, 0, key)


def _kth_largest_key(count_ge, rows, n_sel):
    def bit_body(i, t_u):
        cand_u = t_u | jnp.left_shift(jnp.int32(1), 31 - i)
        cnt = count_ge(cand_u ^ jnp.int32(INT_MIN))
        return jnp.where(cnt >= n_sel, cand_u, t_u)

    t_u = lax.fori_loop(0, 32, bit_body, jnp.zeros((rows, 1), I32))
    return t_u ^ jnp.int32(INT_MIN)


def _tie_cutoff(count_tie_below, need, rows, n_bits):
    def bit_body(i, j_u):
        cand = j_u | jnp.left_shift(jnp.int32(1), n_bits - 1 - i)
        return jnp.where(count_tie_below(cand) < need, cand, j_u)

    return lax.fori_loop(0, n_bits, bit_body, jnp.zeros((rows, 1), I32))


def _mask_add(key, col, t, jcut):
    tie = jnp.where(col <= jcut, 0.0, NEG_INF)
    return jnp.where(key > t, 0.0, jnp.where(key == t, tie, NEG_INF))


def _attn_prompt_kernel(qp_ref, qip_ref, wi_ref, ki2_ref, kb_ref, vb_ref, o_ref,
                        key_scr, madd_scr, s_scr, mx_scr, ls_scr, acc_scr, jc_scr, *, n_sel, n_bits, kc, qb):
    jb = pl.program_id(1)
    nch = ((jb + 1) * qb - 1) // kc + 1
    tiles = kc // LANES
    lane = lax.broadcasted_iota(I32, (qb, LANES), 1)
    row = lax.broadcasted_iota(I32, (qb, LANES), 0)
    qpos = jb * qb + row

    def key_chunk(ref, c):
        return ref[0, pl.ds(pl.multiple_of(c * kc, kc), kc), :]

    def lane_tile(x, i):
        return x[:, i * LANES:(i + 1) * LANES]

    def bcast(x):
        return jnp.broadcast_to(x, (qb, LANES))

    qi_all = qip_ref[0]
    qi_stack = jnp.concatenate([lane_tile(qi_all, h) for h in range(IDX_HEADS)], axis=0)
    w = wi_ref[0]
    wcols = [bcast(w[:, h:h + 1]) for h in range(IDX_HEADS)]

    def score_body(c, carry):
        d = _dot_nt(qi_stack, key_chunk(ki2_ref, c))
        keys = []
        for i in range(tiles):
            sc = jnp.maximum(lane_tile(d[0:qb], i), 0.0) * wcols[0]
            for h in range(1, IDX_HEADS):
                sc = sc + jnp.maximum(lane_tile(d[h * qb:(h + 1) * qb], i), 0.0) * wcols[h]
            col = c * kc + i * LANES + lane
            keys.append(jnp.where(col > qpos, jnp.int32(INT_MIN), _score_key(sc)))
        key_scr[c] = jnp.concatenate(keys, axis=1)
        return carry

    lax.fori_loop(0, nch, score_body, 0)

    rsub = Q_BLOCK
    lane_sub = lax.broadcasted_iota(I32, (rsub, LANES), 1)

    def count(ind, *row_args):
        parts = []
        for r in range(qb // rsub):
            rows = slice(r * rsub, (r + 1) * rsub)
            args = [jnp.broadcast_to(a[rows], (rsub, LANES)) for a in row_args]

            def body(c, acc):
                k = key_scr[c, rows, :]
                for i in range(tiles):
                    acc = acc + ind(lane_tile(k, i), c * kc + i * LANES + lane_sub, *args)
                return acc

            parts.append(lax.fori_loop(0, nch, body, jnp.zeros((rsub, LANES), F32)))
        return jnp.sum(jnp.concatenate(parts, axis=0), axis=1, keepdims=True)

    def count_cmp(cmp, ref_key):
        return count(lambda k, col, rb: jnp.where(cmp(k, rb), 1.0, 0.0), ref_key)

    t = _kth_largest_key(lambda cand: count_cmp(lambda k, c: k >= c, cand), qb, n_sel)
    tb = bcast(t)
    cnt_gt = count_cmp(lambda k, c: k > c, t)
    cnt_ge = count_cmp(lambda k, c: k >= c, t)
    need = n_sel - cnt_gt
    live = t != jnp.int32(INT_MIN)
    excess = jnp.where(jnp.logical_and(live, cnt_ge > n_sel), 1.0, 0.0)
    jc_scr[...] = bcast(jnp.where(live, jnp.int32(2 ** 30), jnp.int32(-1)))

    @pl.when(jnp.max(excess) > 0.0)
    def _():
        def ties_below(cand):
            return count(lambda k, col, tr, cb: jnp.where(k == tr, jnp.where(col < cb, 1.0, 0.0), 0.0), t, cand)

        jcut = _tie_cutoff(ties_below, need, qb, n_bits)
        jc_scr[...] = bcast(jnp.where(live, jcut, jnp.int32(-1)))

    jcb = jc_scr[...]

    def mask_body(c, carry):
        k = key_scr[c]
        madd_scr[c] = jnp.concatenate(
            [_mask_add(lane_tile(k, i), c * kc + i * LANES + lane, tb, jcb) for i in range(tiles)], axis=1)
        return carry

    lax.fori_loop(0, nch, mask_body, 0)

    hpg = N_HEADS // N_KV
    lo = lane < HEAD_DIM
    outs = []
    for g in range(N_KV):
        qg = jnp.concatenate([qp_ref[0, :, (g * hpg + i) * LANES:(g * hpg + i + 1) * LANES] for i in range(hpg)],
                             axis=0)
        mx_scr[...] = jnp.full(mx_scr.shape, NEG_INF, F32)

        def pass1(c, carry):
            s = _dot_nt(qg, key_chunk(kb_ref, c))
            m_add = madd_scr[c]
            for i in range(hpg):
                rows = slice(i * qb, (i + 1) * qb)
                si = s[rows] + m_add
                s_scr[c, rows, :] = si
                best = lane_tile(si, 0)
                for j in range(1, tiles):
                    best = jnp.maximum(best, lane_tile(si, j))
                mx_scr[rows, :] = jnp.maximum(mx_scr[rows, :], best)
            return carry

        lax.fori_loop(0, nch, pass1, 0)
        mx_scr[...] = jnp.broadcast_to(jnp.max(mx_scr[...], axis=1, keepdims=True), mx_scr.shape)
        ls_scr[...] = jnp.zeros(ls_scr.shape, F32)
        acc_scr[...] = jnp.zeros(acc_scr.shape, F32)

        def pass2(c, carry):
            s = s_scr[c]
            mb = mx_scr[...]
            ps = [jnp.exp(lane_tile(s, j) - mb) for j in range(tiles)]
            tot = ps[0]
            for pj in ps[1:]:
                tot = tot + pj
            ls_scr[...] += tot
            p = jnp.concatenate(ps, axis=1).astype(MXU_DTYPE)
            acc_scr[...] += _dot(p, key_chunk(vb_ref, c))
            return carry

        lax.fori_loop(0, nch, pass2, 0)
        o = acc_scr[...] / jnp.sum(ls_scr[...], axis=1, keepdims=True)
        for pair in range(hpg // 2):
            even = o[(2 * pair) * qb:(2 * pair + 1) * qb]
            odd = o[(2 * pair + 1) * qb:(2 * pair + 2) * qb]
            if g == 0:
                outs.append(jnp.where(lo, even, pltpu.roll(odd, HEAD_DIM, 1)))
            else:
                outs.append(jnp.where(lo, pltpu.roll(even, HEAD_DIM, 1), odd))
    for pair, val in enumerate(outs):
        o_ref[0, :, pair * LANES:(pair + 1) * LANES] = val.astype(o_ref.dtype)


def _attn_prompt(qp, qip, wi, ki2, kb, vb, n_sel):
    b, s, _ = qp.shape
    qb = ATT_QUERY_ROWS if s % ATT_QUERY_ROWS == 0 else Q_BLOCK
    kc = ATT_KEY_CHUNK if s % ATT_KEY_CHUNK == 0 else Q_BLOCK
    nq = s // qb
    n_bits = max(1, (s - 1).bit_length())
    blk = lambda i, j: (i, j, 0)
    seq = lambda i, j: (i, 0, 0)
    hq = (N_HEADS // N_KV) * qb
    return pl.pallas_call(
        functools.partial(_attn_prompt_kernel, n_sel=n_sel, n_bits=n_bits, kc=kc, qb=qb),
        out_shape=jax.ShapeDtypeStruct((b, s, N_HEADS * HEAD_DIM), MXU_DTYPE),
        grid=(b, nq),
        in_specs=[pl.BlockSpec((1, qb, qp.shape[2]), blk),
                  pl.BlockSpec((1, qb, qip.shape[2]), blk),
                  pl.BlockSpec((1, qb, LANES), blk),
                  pl.BlockSpec((1, s, LANES), seq),
                  pl.BlockSpec((1, s, LANES), seq),
                  pl.BlockSpec((1, s, LANES), seq)],
        out_specs=pl.BlockSpec((1, qb, N_HEADS * HEAD_DIM), blk),
        scratch_shapes=[pltpu.VMEM((s // kc, qb, kc), I32),
                        pltpu.VMEM((s // kc, qb, kc), F32),
                        pltpu.VMEM((s // kc, hq, kc), F32),
                        pltpu.VMEM((hq, LANES), F32),
                        pltpu.VMEM((hq, LANES), F32),
                        pltpu.VMEM((hq, LANES), F32),
                        pltpu.VMEM((qb, LANES), I32)],
        compiler_params=_cparams(2),
        name="attn_prompt",
    )(qp, qip, wi, ki2, kb, vb)


def _out_kernel(x_ref, yl_ref, ya_ref, mod_ref, g_ref, wo_ref, o_ref, *, lw):
    y = _dot(yl_ref[...], wo_ref[:lw, :]) + _dot(ya_ref[...], wo_ref[lw:, :])
    o_ref[...] = x_ref[...] + mod_ref[0, 5] * _rms(y, g_ref[3:4, :])


def _out_proj(x, y_lru, y_att, mod4, g, wo):
    t, d = x.shape
    lw = y_lru.shape[1]
    tm = _token_tile(t)
    n_tiles = t // tm
    tok = lambda i: (i, 0)
    return pl.pallas_call(
        functools.partial(_out_kernel, lw=lw),
        out_shape=jax.ShapeDtypeStruct((t, d), F32),
        grid=(n_tiles,),
        in_specs=[pl.BlockSpec((tm, d), tok), pl.BlockSpec((tm, lw), tok),
                  pl.BlockSpec((tm, y_att.shape[1]), tok), _mod_spec(mod4, n_tiles),
                  pl.BlockSpec(g.shape, lambda i: (0, 0)), pl.BlockSpec(wo.shape, lambda i: (0, 0))],
        out_specs=pl.BlockSpec((tm, d), tok),
        compiler_params=_cparams(1),
        name="out_proj",
    )(x, y_lru, y_att, mod4, g, wo)


def _attn_sample_kernel(pt_ref, qs_ref, qis_ref, wcol_ref, kin_ref, kn_ref, vn_ref, ckidx_ref, ck_ref, cv_ref,
                        o_ref, kidx_buf, k_buf, v_buf, key_scr, madd_scr, s_scr, sems,
                        *, layer, n_pages, page, ds, n_sel, n_bits, kc):
    bi = pl.program_id(0)
    past = n_pages * page
    total = past + LANES
    n_chunks = past // kc

    streams = ((ckidx_ref, kidx_buf), (ck_ref, k_buf), (cv_ref, v_buf))

    def page_copy(which, p):
        src, dst = streams[which]
        cols = pl.ds(pl.multiple_of(p * page, page), page)
        return pltpu.make_async_copy(src.at[layer, pt_ref[bi, p]], dst.at[:, cols], sems.at[which])

    def start_all(p, carry):
        for which in range(len(streams)):
            page_copy(which, p).start()
        return carry

    lax.fori_loop(0, n_pages, start_all, 0)

    def wait_all(which):
        def body(p, carry):
            page_copy(which, p).wait()
            return carry
        lax.fori_loop(0, n_pages, body, 0)

    kidx_buf[:, past:total] = kin_ref[0]
    k_buf[:, past:total] = kn_ref[0]
    v_buf[:, past:total] = vn_ref[0]
    key_scr[...] = jnp.full(key_scr.shape, INT_MIN, I32)

    qis = qis_ref[0]
    wcol = wcol_ref[0][:, 0:1]
    wait_all(0)

    def scores(lo_col, width):
        d = _dot(qis, kidx_buf[:, lo_col:lo_col + width].astype(MXU_DTYPE))
        wgt = jnp.maximum(d, 0.0) * wcol
        sc = wgt[0:ds]
        for h in range(1, IDX_HEADS):
            sc = sc + wgt[h * ds:(h + 1) * ds]
        return sc

    top = SUBLANES // 2
    half = past // 2
    hw = key_scr.shape[1]

    def place(lo_col):
        return (0, lo_col) if lo_col < half else (top, lo_col - half)

    for c in range(n_chunks):
        r0, c0 = place(c * kc)
        key_scr[r0:r0 + ds, c0:c0 + kc] = _score_key(scores(c * kc, kc))
    sc_new = scores(past, LANES)
    lane_new = lax.broadcasted_iota(I32, (ds, LANES), 1)
    row_new = lax.broadcasted_iota(I32, (ds, LANES), 0)
    r0, c0 = place(past)
    key_scr[r0:r0 + ds, c0:c0 + LANES] = jnp.where(lane_new > row_new, jnp.int32(INT_MIN), _score_key(sc_new))

    rows = key_scr.shape[0]
    lane = lax.broadcasted_iota(I32, (rows, LANES), 1)
    col0 = lane + jnp.where(lax.broadcasted_iota(I32, (rows, LANES), 0) >= top, half, 0)

    def count(ind, *row_args):
        args = [jnp.broadcast_to(a, (rows, LANES)) for a in row_args]
        acc = jnp.zeros((rows, LANES), F32)
        for i in range(hw // LANES):
            acc = acc + ind(key_scr[:, i * LANES:(i + 1) * LANES], col0 + i * LANES, *args)
        acc = acc + pltpu.roll(acc, top, 0)
        return jnp.sum(acc, axis=1, keepdims=True)

    t = _kth_largest_key(lambda cand: count(lambda k, col, cb: jnp.where(k >= cb, 1.0, 0.0), cand), rows, n_sel)
    cnt_gt = count(lambda k, col, tb: jnp.where(k > tb, 1.0, 0.0), t)
    need = n_sel - cnt_gt
    live = t != jnp.int32(INT_MIN)
    jcut = _tie_cutoff(
        lambda cand: count(lambda k, col, tb, cb: jnp.where(k == tb, jnp.where(col < cb, 1.0, 0.0), 0.0), t, cand),
        need, rows, n_bits)
    jcut = jnp.where(live, jcut, jnp.int32(-1))
    tb, jb = jnp.broadcast_to(t, (rows, LANES)), jnp.broadcast_to(jcut, (rows, LANES))
    for i in range(hw // LANES):
        cols = slice(i * LANES, (i + 1) * LANES)
        madd_scr[:, cols] = _mask_add(key_scr[:, cols], col0 + i * LANES, tb, jb)

    qs = qs_ref[0]
    nrow = qs.shape[0]

    def row_mask(lo_col, width):
        r0, c0 = place(lo_col)
        return jnp.concatenate(
            [jnp.broadcast_to(madd_scr[r0 + tt:r0 + tt + 1, c0:c0 + width], (N_HEADS, width)) for tt in range(ds)],
            axis=0)

    spans = [(c * kc, kc) for c in range(n_chunks)] + [(past, LANES)]
    wait_all(1)
    m = jnp.full((nrow, 1), NEG_INF, F32)
    for lo_col, width in spans:
        s = _dot(qs, k_buf[:, lo_col:lo_col + width].astype(MXU_DTYPE)) + row_mask(lo_col, width)
        s_scr[:, lo_col:lo_col + width] = s
        m = jnp.maximum(m, jnp.max(s, axis=1, keepdims=True))
    wait_all(2)
    l = jnp.zeros((nrow, 1), F32)
    acc = jnp.zeros((nrow, LANES), F32)
    for lo_col, width in spans:
        p = jnp.exp(s_scr[:, lo_col:lo_col + width] - m)
        l = l + jnp.sum(p, axis=1, keepdims=True)
        acc = acc + _dot_nt(p.astype(MXU_DTYPE), v_buf[:, lo_col:lo_col + width].astype(MXU_DTYPE))
    o_ref[0] = acc / l


def _attn_sample(page_table, qs, qis, wcol, ki_new, k_new, v_new, cache_kidx, cache_k2, cache_v2, layer, n_sel, ds):
    db, n_pages = page_table.shape
    page = cache_kidx.shape[3]
    past = n_pages * page
    total = past + LANES
    kc = 2048 if past % 4096 == 0 else page
    assert (past // kc) % 2 == 0 and 2 * ds <= SUBLANES
    n_bits = max(1, (total - 1).bit_length())
    per = lambda i, pt: (i, 0, 0)
    any_spec = pl.BlockSpec(memory_space=pl.ANY)
    grid_spec = pltpu.PrefetchScalarGridSpec(
        num_scalar_prefetch=1,
        grid=(db,),
        in_specs=[pl.BlockSpec((1,) + qs.shape[1:], per), pl.BlockSpec((1,) + qis.shape[1:], per),
                  pl.BlockSpec((1,) + wcol.shape[1:], per), pl.BlockSpec((1,) + ki_new.shape[1:], per),
                  pl.BlockSpec((1,) + k_new.shape[1:], per), pl.BlockSpec((1,) + v_new.shape[1:], per),
                  any_spec, any_spec, any_spec],
        out_specs=pl.BlockSpec((1, qs.shape[1], LANES), per),
        scratch_shapes=[pltpu.VMEM((IDX_DIM, total), F32),
                        pltpu.VMEM((LANES, total), F32),
                        pltpu.VMEM((LANES, total), F32),
                        pltpu.VMEM((SUBLANES, past // 2 + LANES), I32),
                        pltpu.VMEM((SUBLANES, past // 2 + LANES), F32),
                        pltpu.VMEM((qs.shape[1], total), F32),
                        pltpu.SemaphoreType.DMA((3,))])
    return pl.pallas_call(
        functools.partial(_attn_sample_kernel, layer=layer, n_pages=n_pages, page=page, ds=ds, n_sel=n_sel,
                          n_bits=n_bits, kc=kc),
        out_shape=jax.ShapeDtypeStruct((db, qs.shape[1], LANES), F32),
        grid_spec=grid_spec,
        compiler_params=_cparams(1),
        name="attn_sample",
    )(page_table, qs, qis, wcol, ki_new, k_new, v_new, cache_kidx, cache_k2, cache_v2)


def _rope_tables(pos):
    half = ROPE_DIM // 2
    freqs = ROPE_THETA ** (-(jnp.arange(half, dtype=F32) / half))
    ang = pos.astype(F32)[:, None] * freqs[None, :]
    cos, sin = jnp.cos(ang), jnp.sin(ang)
    rest = HEAD_DIM - ROPE_DIM
    c64 = jnp.concatenate([cos, cos, jnp.ones((pos.shape[0], rest), F32)], axis=1)
    s64 = jnp.concatenate([-sin, sin, jnp.zeros((pos.shape[0], rest), F32)], axis=1)
    return jnp.tile(c64, (1, LANES // HEAD_DIM)), jnp.tile(s64, (1, LANES // HEAD_DIM))


def _block_diag(w):
    nb, n, _ = w.shape
    eye = jnp.eye(nb, dtype=w.dtype)
    return (eye[:, None, :, None] * w[:, :, None, :]).reshape(nb * n, nb * n)


def kernel(x_prompt, x_sample, c_prompt, c_sample, cache_k, cache_v, cache_kidx, state_h, state_conv, page_table,
           w_ada, b_ada, norm_g, ffn_w_gu, ffn_w_down, w_in, conv_w, conv_b, lru_wa, lru_ba, lru_wx, lru_bx,
           lru_lambda, w_out):
    b, s, d = x_prompt.shape
    db, ds, _ = x_sample.shape
    depth = w_ada.shape[0]
    lw = lru_lambda.shape[1]
    n_pool, page = cache_k.shape[1], cache_k.shape[2]
    past = page_table.shape[1] * page
    kv_w = N_KV * HEAD_DIM
    assert kv_w == LANES and 2 * IDX_DIM == LANES and s % Q_BLOCK == 0 and ds <= SUBLANES

    cos_p, sin_p = _rope_tables(jnp.arange(s, dtype=I32))
    cos_s, sin_s = _rope_tables(past + jnp.arange(ds, dtype=I32))
    cos_s, sin_s = jnp.tile(cos_s, (db, 1)), jnp.tile(sin_s, (db, 1))

    n_c = b + db
    c_all = jnp.concatenate([c_prompt, c_sample, jnp.zeros((-n_c % SUBLANES, d), F32)], axis=0)
    assert page % LANES == 0
    cache_kidx_t = cache_kidx.transpose(0, 1, 3, 2)
    cache_k_t = cache_k.transpose(0, 1, 3, 4, 2).reshape(depth, n_pool, kv_w, page)
    cache_v_t = cache_v.transpose(0, 1, 3, 4, 2).reshape(depth, n_pool, kv_w, page)

    def new_cols(a):
        a = a.reshape(db, ds, a.shape[-1]).transpose(0, 2, 1)
        return jnp.pad(a, ((0, 0), (0, 0), (0, LANES - ds)))

    xp = x_prompt.reshape(b * s, d)
    xs = x_sample.reshape(db * ds, d)
    n_sel_p = min(TOP_K, s // 4)
    n_sel_s = min(TOP_K, (past + ds) // 4)
    q_w = N_HEADS * HEAD_DIM
    qi_w = IDX_HEADS * IDX_DIM
    col_ki = 2 * lw + q_w + 2 * kv_w + qi_w

    outs = {name: [] for name in ("kp", "vp", "kip", "hp", "cp", "ks", "vs", "kis", "hs", "cs")}
    for l in range(depth):
        mod = _ada_mod(c_all, w_ada[l], b_ada[l])
        mod_p = mod[:b].reshape(b, N_MOD, 1, d)
        mod_s = jnp.repeat(mod[b:n_c].reshape(db, N_MOD, d), ds, axis=0).transpose(1, 0, 2)[None]
        g = norm_g[l]
        wgu = ffn_w_gu[l].astype(MXU_DTYPE)
        wdn = ffn_w_down[l].astype(MXU_DTYPE)
        w_ki = w_in[l][:, col_ki:col_ki + IDX_DIM]
        w_pad = jnp.concatenate(
            [w_in[l][:, :col_ki], w_ki, w_ki, w_in[l][:, col_ki + IDX_DIM:],
             jnp.zeros((d, LANES - IDX_HEADS), F32)], axis=1).astype(MXU_DTYPE)
        wo = w_out[l].astype(MXU_DTYPE)
        wa_bd = _block_diag(lru_wa[l]).astype(MXU_DTYPE)
        wx_bd = _block_diag(lru_wx[l]).astype(MXU_DTYPE)
        lru_args = (conv_w[l], conv_b[l].reshape(1, lw), wa_bd, lru_ba[l].reshape(1, lw), wx_bd,
                    lru_bx[l].reshape(1, lw), lru_lambda[l].reshape(1, lw))

        xp = _ffn(xp, mod_p, g, wgu[0], wdn[0], 0, 0)
        xl, gl, qp, k, v, kb, vb, qip, ki, ki2, wi = _proj(xp, mod_p, g, w_pad, cos_p, sin_p, lw, s // _token_tile(b * s))
        r3 = lambda a: a.reshape(b, s, a.shape[-1])
        y_lru, h_t, nbuf = _lru_prompt(r3(xl), r3(gl), jnp.zeros((b, CONV_W - 1, lw), F32), jnp.zeros((b, lw), F32),
                                       *lru_args)
        y_att = _attn_prompt(r3(qp), r3(qip), r3(wi), r3(ki2), r3(kb), r3(vb), n_sel_p)
        xp = _out_proj(xp, y_lru.reshape(b * s, lw), y_att.reshape(b * s, q_w), mod_p, g, wo)
        xp = _ffn(xp, mod_p, g, wgu[1], wdn[1], 6, 4)
        outs["kp"].append(k.reshape(b, s, N_KV, HEAD_DIM))
        outs["vp"].append(v.reshape(b, s, N_KV, HEAD_DIM))
        outs["kip"].append(ki.reshape(b, s, IDX_DIM))
        outs["hp"].append(h_t.reshape(b, lw))
        outs["cp"].append(nbuf)

        xs = _ffn(xs, mod_s, g, wgu[0], wdn[0], 0, 0)
        xl, gl, qp, k, v, kb, vb, qip, ki, ki2, wi = _proj(xs, mod_s, g, w_pad, cos_s, sin_s, lw, 1)
        tmaj = lambda a: a.reshape(db, ds, a.shape[-1]).transpose(1, 0, 2)
        y_lru, h_t, nbuf = _lru_sample(tmaj(xl), tmaj(gl), state_conv[l].transpose(1, 0, 2), state_h[l], *lru_args)
        qs = qp.reshape(db, ds * N_HEADS, LANES)
        qis = qip.reshape(db, ds, IDX_HEADS, 2, IDX_DIM)
        qis = jnp.stack([qis[:, :, h, h % 2] for h in range(IDX_HEADS)], axis=1).reshape(db, IDX_HEADS * ds, IDX_DIM)
        wcol = wi.reshape(db, ds, LANES)[:, :, :IDX_HEADS].transpose(0, 2, 1).reshape(db, IDX_HEADS * ds, 1)
        wcol = jnp.broadcast_to(wcol, (db, IDX_HEADS * ds, LANES))
        o_s = _attn_sample(page_table, qs, qis, wcol, new_cols(ki), new_cols(k), new_cols(v),
                           cache_kidx_t, cache_k_t, cache_v_t, l, n_sel_s, ds)
        o_s = o_s.reshape(db, ds, N_HEADS, N_KV, HEAD_DIM)
        hpg = N_HEADS // N_KV
        y_att = jnp.stack([o_s[:, :, h, h // hpg] for h in range(N_HEADS)], axis=2).reshape(db * ds, q_w)
        xs = _out_proj(xs, y_lru.transpose(1, 0, 2).reshape(db * ds, lw), y_att.astype(MXU_DTYPE), mod_s, g, wo)
        xs = _ffn(xs, mod_s, g, wgu[1], wdn[1], 6, 4)
        outs["ks"].append(k.reshape(db, ds, N_KV, HEAD_DIM))
        outs["vs"].append(v.reshape(db, ds, N_KV, HEAD_DIM))
        outs["kis"].append(ki.reshape(db, ds, IDX_DIM))
        outs["hs"].append(h_t)
        outs["cs"].append(nbuf.transpose(1, 0, 2))

    st = lambda name: jnp.stack(outs[name])
    return (xp.reshape(b, s, d), xs.reshape(db, ds, d), st("kp"), st("vp"), st("kip"), st("hp"), st("cp"),
            st("ks"), st("vs"), st("kis"), st("hs"), st("cs"))
```

```python
import functools
import math

import jax
import jax.numpy as jnp
from jax import lax
from jax.experimental import pallas as pl
from jax.experimental.pallas import tpu as pltpu

F32 = jnp.float32
I32 = jnp.int32
I16 = jnp.int16
MXU_DTYPE = jnp.bfloat16

LRU_BLOCKS = 8
CONV_W = 4
LRU_C = 8.0
N_HEADS = 8
HEAD_DIM = 64
N_KV = 2
ROPE_DIM = HEAD_DIM // 4
ROPE_THETA = 500000.0
IDX_HEADS = 4
IDX_DIM = 64
TOP_K = 256
Q_BLOCK = 128
N_MOD = 9
RMS_EPS = 1e-6
LOG2E = math.log2(math.e)

LANES = 128
SUBLANES = 8
PACKED_SUBLANES = 16
TOKEN_TILE = 512
ATT_QUERIES = 256
VMEM_LIMIT = 56 * 1024 * 1024
INT_MIN = -2147483648
HALF_RANGE = 32768
NEG_INF = float("-inf")


def _cparams(n_axes):
    return pltpu.CompilerParams(dimension_semantics=("arbitrary",) * n_axes,
                                vmem_limit_bytes=VMEM_LIMIT)


def _rms(x, g):
    ms = jnp.mean(x * x, axis=-1, keepdims=True)
    return x * lax.rsqrt(ms + RMS_EPS) * g


def _dot(a, b):
    return jnp.dot(a, b, preferred_element_type=F32)


def _dot_nt(a, b):
    return lax.dot_general(a, b, (((1,), (1,)), ((), ())), preferred_element_type=F32)


def _mod_kernel(c_ref, w_ref, b_ref, o_ref):
    o_ref[...] = jnp.dot(c_ref[...], w_ref[...], preferred_element_type=F32,
                         precision=lax.Precision.HIGHEST) + b_ref[...]


def _ada_mod(c_all, w, b):
    rows, d = c_all.shape
    n = w.shape[1]
    tn = d
    return pl.pallas_call(
        _mod_kernel,
        out_shape=jax.ShapeDtypeStruct((rows, n), F32),
        grid=(n // tn,),
        in_specs=[pl.BlockSpec((rows, d), lambda i: (0, 0)),
                  pl.BlockSpec((d, tn), lambda i: (0, i)),
                  pl.BlockSpec((1, tn), lambda i: (0, i))],
        out_specs=pl.BlockSpec((rows, tn), lambda i: (0, i)),
        compiler_params=_cparams(1),
        name="ada_mod",
    )(c_all, w, b.reshape(1, n))


def _ffn_kernel(x_ref, mod_ref, g_ref, wgu_ref, wdn_ref, o_ref, *, m0, gi, d_ff):
    x = x_ref[...]
    sh, sc, gt = mod_ref[0, m0], mod_ref[0, m0 + 1], mod_ref[0, m0 + 2]
    h = (_rms(x, g_ref[gi:gi + 1, :]) * (1.0 + sc) + sh).astype(MXU_DTYPE)
    g = _dot(h, wgu_ref[:, :d_ff])
    u = _dot(h, wgu_ref[:, d_ff:])
    a = (g * jax.nn.sigmoid(g) * u).astype(MXU_DTYPE)
    f = _dot(a, wdn_ref[...])
    o_ref[...] = x + 0.5 * gt * _rms(f, g_ref[gi + 1:gi + 2, :])


def _token_tile(t):
    return TOKEN_TILE if t % TOKEN_TILE == 0 else t


def _mod_spec(mod4, n_tiles):
    nb, _, r, d = mod4.shape
    per = n_tiles // nb
    return pl.BlockSpec((1, N_MOD, r, d), lambda i: (i // per, 0, 0, 0))


def _ffn(x, mod4, g, wgu, wdn, m0, gi):
    t, d = x.shape
    d_ff = wdn.shape[0]
    tm = _token_tile(t)
    n_tiles = t // tm
    return pl.pallas_call(
        functools.partial(_ffn_kernel, m0=m0, gi=gi, d_ff=d_ff),
        out_shape=jax.ShapeDtypeStruct((t, d), F32),
        grid=(n_tiles,),
        in_specs=[pl.BlockSpec((tm, d), lambda i: (i, 0)),
                  _mod_spec(mod4, n_tiles),
                  pl.BlockSpec(g.shape, lambda i: (0, 0)),
                  pl.BlockSpec(wgu.shape, lambda i: (0, 0)),
                  pl.BlockSpec(wdn.shape, lambda i: (0, 0))],
        out_specs=pl.BlockSpec((tm, d), lambda i: (i, 0)),
        compiler_params=_cparams(1),
        name="ffn",
    )(x, mod4, g, wgu, wdn)


def _rope(x, cos, sin):
    lane = lax.broadcasted_iota(I32, x.shape, 1)
    first = (lane & (HEAD_DIM - 1)) < (ROPE_DIM // 2)
    partner = jnp.where(first, pltpu.roll(x, LANES - ROPE_DIM // 2, 1), pltpu.roll(x, ROPE_DIM // 2, 1))
    return x * cos + partner * sin


def _proj_kernel(x_ref, mod_ref, g_ref, w_ref, cos_ref, sin_ref,
                 xl_ref, gl_ref, qp_ref, k_ref, v_ref, kb_ref, vbt_ref, qip_ref, ki_ref, ki2_ref, wi_ref, wit_ref,
                 *, lw):
    x = x_ref[...]
    h = (_rms(x, g_ref[2:3, :]) * (1.0 + mod_ref[0, 4]) + mod_ref[0, 3]).astype(MXU_DTYPE)
    p = _dot(h, w_ref[...])
    cos, sin = cos_ref[...], sin_ref[...]
    lane = lax.broadcasted_iota(I32, cos.shape, 1)
    lo = lane < HEAD_DIM

    o = 0
    xl_ref[...] = p[:, o:o + lw]
    o += lw
    gl_ref[...] = p[:, o:o + lw]
    o += lw
    q_scale = HEAD_DIM ** -0.5 * LOG2E
    for pair in range(N_HEADS // 2):
        qr = _rope(p[:, o + pair * LANES:o + (pair + 1) * LANES], cos, sin) * q_scale
        qr_sw = pltpu.roll(qr, HEAD_DIM, 1)
        for half in range(2):
            hd = 2 * pair + half
            grp = hd // (N_HEADS // N_KV)
            src = qr if half == grp else qr_sw
            slab = jnp.where(lo if grp == 0 else jnp.logical_not(lo), src, 0.0)
            qp_ref[:, hd * LANES:(hd + 1) * LANES] = slab.astype(qp_ref.dtype)
    o += N_HEADS * HEAD_DIM
    k = _rope(p[:, o:o + LANES], cos, sin)
    k_ref[...] = k
    kb_ref[...] = k.astype(kb_ref.dtype)
    o += LANES
    v = p[:, o:o + LANES]
    v_ref[...] = v
    vbt_ref[0] = v.T.astype(vbt_ref.dtype)
    o += LANES
    for pair in range(IDX_HEADS // 2):
        qi = _rope(p[:, o + pair * LANES:o + (pair + 1) * LANES], cos, sin)
        for half in range(2):
            hd = 2 * pair + half
            slab = jnp.where(lo if half == 0 else jnp.logical_not(lo), qi, 0.0)
            qip_ref[:, hd * LANES:(hd + 1) * LANES] = slab.astype(qip_ref.dtype)
    o += IDX_HEADS * IDX_DIM
    ki2 = _rope(p[:, o:o + LANES], cos, sin)
    ki_ref[...] = ki2[:, :IDX_DIM]
    ki2_ref[...] = ki2.astype(ki2_ref.dtype)
    o += LANES
    wi = p[:, o:o + LANES] * (IDX_DIM ** -0.5 * IDX_HEADS ** -0.5)
    wi_ref[...] = wi
    wit_ref[...] = wi.T[:SUBLANES, :]


def _proj(x, mod4, g, w_pad, cos, sin, lw, pos_tiles):
    t, d = x.shape
    tm = _token_tile(t)
    n_tiles = t // tm
    tok = lambda i: (i, 0)
    pos = lambda i: (i % pos_tiles, 0)
    row_major = [(lw, F32), (lw, F32), (N_HEADS * LANES, MXU_DTYPE), (LANES, F32), (LANES, F32),
                 (LANES, MXU_DTYPE), None, (IDX_HEADS * LANES, MXU_DTYPE),
                 (IDX_DIM, F32), (LANES, MXU_DTYPE), (LANES, F32), None]
    out_shape, out_specs = [], []
    for idx, entry in enumerate(row_major):
        if entry is not None:
            out_shape.append(jax.ShapeDtypeStruct((t, entry[0]), entry[1]))
            out_specs.append(pl.BlockSpec((tm, entry[0]), tok))
        elif idx == 6:
            out_shape.append(jax.ShapeDtypeStruct((n_tiles, LANES, tm), MXU_DTYPE))
            out_specs.append(pl.BlockSpec((1, LANES, tm), lambda i: (i, 0, 0)))
        else:
            out_shape.append(jax.ShapeDtypeStruct((SUBLANES, t), F32))
            out_specs.append(pl.BlockSpec((SUBLANES, tm), lambda i: (0, i)))
    return pl.pallas_call(
        functools.partial(_proj_kernel, lw=lw),
        out_shape=out_shape,
        grid=(n_tiles,),
        in_specs=[pl.BlockSpec((tm, d), tok),
                  _mod_spec(mod4, n_tiles),
                  pl.BlockSpec(g.shape, lambda i: (0, 0)),
                  pl.BlockSpec(w_pad.shape, lambda i: (0, 0)),
                  pl.BlockSpec((tm, LANES), pos),
                  pl.BlockSpec((tm, LANES), pos)],
        out_specs=out_specs,
        compiler_params=_cparams(1),
        name="mixer_proj",
    )(x, mod4, g, w_pad, cos, sin)


def _gelu_tanh(x):
    return x * (0.5 * (1.0 + jnp.tanh(0.7978845608028654 * (x + 0.044715 * (x * x * x)))))


def _lru_coeffs(xc, wa_ref, ba_ref, wx_ref, bx_ref, lam_ref):
    xcb = xc.astype(MXU_DTYPE)
    r = jax.nn.sigmoid(_dot(xcb, wa_ref[...]) + ba_ref[...])
    i = jax.nn.sigmoid(_dot(xcb, wx_ref[...]) + bx_ref[...])
    nl = -lam_ref[...]
    softplus = jnp.maximum(nl, 0.0) + jnp.log1p(jnp.exp(-jnp.abs(nl)))
    log_a = (-LRU_C) * r * softplus
    a = jnp.exp(log_a)
    one_minus_a2 = -jnp.tanh(log_a) * (a * a + 1.0)
    return a, jnp.sqrt(one_minus_a2) * (i * xc)


def _lru_prompt_kernel(xl_ref, gl_ref, buf_ref, h0_ref, cw_ref, cb_ref, wa_ref, ba_ref, wx_ref, bx_ref, lam_ref,
                       y_ref, ht_ref, nb_ref, xs_scr, a_scr, b_scr, h_scr, *, ts):
    tj = pl.program_id(1)
    pad = SUBLANES
    ng = ts // SUBLANES

    @pl.when(tj == 0)
    def _():
        xs_scr[0:pad, :] = jnp.zeros((pad, xs_scr.shape[1]), F32)
        xs_scr[pad - (CONV_W - 1):pad, :] = buf_ref[0]
        h_scr[...] = h0_ref[0]

    x = xl_ref[0]
    xs_scr[pad:pad + ts, :] = x
    xc = cb_ref[...] + x * cw_ref[CONV_W - 1:CONV_W, :]
    for j in range(1, CONV_W):
        xc = xc + xs_scr[pad - j:pad - j + ts, :] * cw_ref[CONV_W - 1 - j:CONV_W - j, :]

    a, b = _lru_coeffs(xc, wa_ref, ba_ref, wx_ref, bx_ref, lam_ref)
    lw = a.shape[1]
    a = a.reshape(ng, SUBLANES, lw)
    b = b.reshape(ng, SUBLANES, lw)
    sub = lax.broadcasted_iota(I32, (ng, SUBLANES, lw), 1)
    d = 1
    while d < SUBLANES:
        keep = sub >= d
        a_sh = jnp.where(keep, pltpu.roll(a, d, 1), 1.0)
        b_sh = jnp.where(keep, pltpu.roll(b, d, 1), 0.0)
        b = a * b_sh + b
        a = a * a_sh
        d *= 2
    a_scr[...] = a
    b_scr[...] = b

    def carry(gidx, h):
        hg = a_scr[gidx] * h + b_scr[gidx]
        b_scr[gidx] = hg
        return hg[SUBLANES - 1:SUBLANES, :]

    h_last = lax.fori_loop(0, ng, carry, h_scr[...], unroll=8)
    h_scr[...] = h_last
    hs = b_scr[...].reshape(ts, lw)
    y_ref[0] = (hs * _gelu_tanh(gl_ref[0])).astype(y_ref.dtype)
    xs_scr[0:pad, :] = xs_scr[ts:ts + pad, :]

    @pl.when(tj == pl.num_programs(1) - 1)
    def _():
        ht_ref[0] = h_last
        nb_ref[0] = xs_scr[pad - (CONV_W - 1):pad, :]


def _lru_prompt(xl, gl, buf0, h0, cw, cb, wa_bd, ba, wx_bd, bx, lam):
    b, s, lw = xl.shape
    ts = _token_tile(s)
    seq = lambda i, j: (i, j, 0)
    per_b = lambda i, j: (i, 0, 0)
    const = lambda i, j: (0, 0)
    row = pl.BlockSpec((1, lw), const)
    return pl.pallas_call(
        functools.partial(_lru_prompt_kernel, ts=ts),
        out_shape=[jax.ShapeDtypeStruct((b, s, lw), MXU_DTYPE),
                   jax.ShapeDtypeStruct((b, 1, lw), F32),
                   jax.ShapeDtypeStruct((b, CONV_W - 1, lw), F32)],
        grid=(b, s // ts),
        in_specs=[pl.BlockSpec((1, ts, lw), seq), pl.BlockSpec((1, ts, lw), seq),
                  pl.BlockSpec((1, CONV_W - 1, lw), per_b), pl.BlockSpec((1, 1, lw), per_b),
                  pl.BlockSpec((CONV_W, lw), const), row,
                  pl.BlockSpec((lw, lw), const), row, pl.BlockSpec((lw, lw), const), row, row],
        out_specs=[pl.BlockSpec((1, ts, lw), seq), pl.BlockSpec((1, 1, lw), per_b),
                   pl.BlockSpec((1, CONV_W - 1, lw), per_b)],
        scratch_shapes=[pltpu.VMEM((ts + SUBLANES, lw), F32),
                        pltpu.VMEM((ts // SUBLANES, SUBLANES, lw), F32),
                        pltpu.VMEM((ts // SUBLANES, SUBLANES, lw), F32),
                        pltpu.VMEM((1, lw), F32)],
        compiler_params=_cparams(2),
        name="rglru_prompt",
    )(xl, gl, buf0, h0.reshape(b, 1, lw), cw, cb, wa_bd, ba, wx_bd, bx, lam)


def _lru_sample_kernel(xl_ref, gl_ref, buf_ref, h0_ref, cw_ref, cb_ref, wa_ref, ba_ref, wx_ref, bx_ref, lam_ref,
                       y_ref, ht_ref, nb_ref, *, ds):
    rows = [buf_ref[j] for j in range(CONV_W - 1)] + [xl_ref[t] for t in range(ds)]
    xcs = []
    for t in range(ds):
        xc = cb_ref[...]
        for j in range(CONV_W):
            xc = xc + rows[t + j] * cw_ref[j:j + 1, :]
        xcs.append(xc)
    db = xcs[0].shape[0]
    a, b = _lru_coeffs(jnp.concatenate(xcs, axis=0), wa_ref, ba_ref, wx_ref, bx_ref, lam_ref)
    h = h0_ref[...]
    for t in range(ds):
        h = a[t * db:(t + 1) * db] * h + b[t * db:(t + 1) * db]
        y_ref[t] = (h * _gelu_tanh(gl_ref[t])).astype(y_ref.dtype)
    ht_ref[...] = h
    for j in range(CONV_W - 1):
        nb_ref[j] = rows[ds + j]


def _lru_sample(xl_t, gl_t, buf_t, h0, cw, cb, wa_bd, ba, wx_bd, bx, lam):
    ds, db, lw = xl_t.shape
    full = lambda a: pl.BlockSpec(a.shape, lambda i, n=a.ndim: (0,) * n)
    args = (xl_t, gl_t, buf_t, h0, cw, cb, wa_bd, ba, wx_bd, bx, lam)
    outs = [jax.ShapeDtypeStruct((ds, db, lw), MXU_DTYPE), jax.ShapeDtypeStruct((db, lw), F32),
            jax.ShapeDtypeStruct((CONV_W - 1, db, lw), F32)]
    return pl.pallas_call(
        functools.partial(_lru_sample_kernel, ds=ds),
        out_shape=outs,
        grid=(1,),
        in_specs=[full(a) for a in args],
        out_specs=[full(o) for o in outs],
        compiler_params=_cparams(1),
        name="rglru_sample",
    )(*args)


def _score_key(sc):
    bits = lax.bitcast_convert_type(sc, I32)
    key = jnp.where(bits < 0, bits ^ jnp.int32(0x7FFFFFFF), bits)
    return jnp.where(sc == 0.0, 0, key)


def _kth_largest_key(count_ge, shape, n_sel):
    def bit_body(i, t_u):
        cand_u = t_u | jnp.left_shift(jnp.int32(1), 31 - i)
        cnt = count_ge(cand_u ^ jnp.int32(INT_MIN))
        return jnp.where(cnt >= n_sel, cand_u, t_u)

    t_u = lax.fori_loop(0, 32, bit_body, jnp.zeros(shape, I32))
    return t_u ^ jnp.int32(INT_MIN)


def _tie_cutoff(count_tie_below, need, shape, n_bits):
    def bit_body(i, j_u):
        cand = j_u | jnp.left_shift(jnp.int32(1), n_bits - 1 - i)
        return jnp.where(count_tie_below(cand) < need, cand, j_u)

    return lax.fori_loop(0, n_bits, bit_body, jnp.zeros(shape, I32))


def _mask_add(key, idx, t, jcut):
    tie = jnp.where(idx <= jcut, 0.0, NEG_INF)
    return jnp.where(key > t, 0.0, jnp.where(key == t, tie, NEG_INF))


def _fold_rows(x, group, op):
    slabs = [x[i * group:(i + 1) * group] for i in range(x.shape[0] // group)]
    chains = slabs[:4]
    for i, slab in enumerate(slabs[4:]):
        chains[i % len(chains)] = op(chains[i % len(chains)], slab)
    while len(chains) > 1:
        chains = [op(chains[i], chains[i + 1]) if i + 1 < len(chains) else chains[i]
                  for i in range(0, len(chains), 2)]
    return chains[0]


def _attn_prompt_kernel(qp_ref, qip_ref, wit_ref, ki2_ref, kb_ref, vbt_ref, o_ref,
                        key_scr, hi_scr, lo_scr, s0_scr, s1_scr, mx0_scr, mx1_scr, ls_scr, acc_scr, jc_scr,
                        *, n_sel, n_bits, kc, qb):
    s_scr, mx_scr = (s0_scr, s1_scr), (mx0_scr, mx1_scr)
    jb = pl.program_id(1)
    nch = ((jb + 1) * qb - 1) // kc + 1
    hpg = N_HEADS // N_KV
    n = hpg * qb

    def key_chunk(ref, c):
        return ref[0, pl.ds(pl.multiple_of(c * kc, kc), kc), :]

    def key_index(c):
        return c * kc + lax.broadcasted_iota(I32, (kc, qb), 0)

    qpos = jb * qb + lax.broadcasted_iota(I32, (kc, qb), 1)

    qi_all = qip_ref[0]
    qi_stack = jnp.concatenate([qi_all[:, h * LANES:(h + 1) * LANES] for h in range(IDX_HEADS)], axis=0)
    wt = wit_ref[...]

    def score_body(c, carry):
        d = _dot_nt(key_chunk(ki2_ref, c), qi_stack)
        sc = jnp.maximum(d[:, 0:qb], 0.0) * wt[0:1, :]
        for h in range(1, IDX_HEADS):
            sc = sc + jnp.maximum(d[:, h * qb:(h + 1) * qb], 0.0) * wt[h:h + 1, :]
        key = jnp.where(key_index(c) > qpos, jnp.int32(INT_MIN), _score_key(sc))
        key_scr[c] = key
        hi_scr[c] = lax.shift_right_arithmetic(key, 16).astype(I16)
        lo_scr[c] = ((key & 0xFFFF) - HALF_RANGE).astype(I16)
        return carry

    lax.fori_loop(0, nch, score_body, 0)

    def rows16(x):
        return jnp.broadcast_to(x, (PACKED_SUBLANES, qb)).astype(I16)

    def count16(src, ind, *args):
        def body(c, accs):
            v = src[c]
            accs = list(accs)
            for i in range(kc // PACKED_SUBLANES):
                j = i % len(accs)
                accs[j] = accs[j] + ind(v[i * PACKED_SUBLANES:(i + 1) * PACKED_SUBLANES], *args)
            return tuple(accs)

        zero = jnp.zeros((PACKED_SUBLANES, qb), I16)
        accs = lax.fori_loop(0, nch, body, (zero, zero, zero, zero))
        acc = (accs[0] + accs[1]) + (accs[2] + accs[3])
        return jnp.sum(acc.astype(I32), axis=0, keepdims=True)

    one16, zero16 = jnp.int16(1), jnp.int16(0)

    def digit_search(src, need):
        def bit_body(i, d_u):
            cand_u = d_u | jnp.left_shift(jnp.int32(1), 15 - i)
            cnt = count16(src, lambda v, c: jnp.where(v >= c, one16, zero16), rows16(cand_u - HALF_RANGE))
            return jnp.where(cnt >= need, cand_u, d_u)

        return lax.fori_loop(0, 16, bit_body, jnp.zeros((1, qb), I32)) - HALF_RANGE

    t_hi = digit_search(hi_scr, n_sel)
    t_hi16 = rows16(t_hi)
    cnt_hi_gt = count16(hi_scr, lambda v, c: jnp.where(v > c, one16, zero16), t_hi16)

    def low_digit_body(c, carry):
        hi, lo = hi_scr[c], lo_scr[c]
        out = []
        for i in range(kc // PACKED_SUBLANES):
            rows = slice(i * PACKED_SUBLANES, (i + 1) * PACKED_SUBLANES)
            out.append(jnp.where(hi[rows] == t_hi16, lo[rows], jnp.int16(-HALF_RANGE)))
        lo_scr[c] = jnp.concatenate(out, axis=0)
        return carry

    lax.fori_loop(0, nch, low_digit_body, 0)
    t_lo = digit_search(lo_scr, n_sel - cnt_hi_gt)
    t_lo16 = rows16(t_lo)
    cnt_gt = cnt_hi_gt + count16(lo_scr, lambda v, c: jnp.where(v > c, one16, zero16), t_lo16)
    cnt_ge = cnt_hi_gt + count16(lo_scr, lambda v, c: jnp.where(v >= c, one16, zero16), t_lo16)
    t = t_hi * (2 * HALF_RANGE) + (t_lo + HALF_RANGE)
    need = n_sel - cnt_gt
    live = t != jnp.int32(INT_MIN)
    excess = jnp.where(jnp.logical_and(live, cnt_ge > n_sel), 1.0, 0.0)
    jc_scr[...] = jnp.broadcast_to(jnp.where(live, jnp.int32(2 ** 30), jnp.int32(-1)), jc_scr.shape)

    @pl.when(jnp.max(excess) > 0.0)
    def _():
        def ties_below(cand):
            def body(c, acc):
                hit = jnp.where(key_scr[c] == t, jnp.where(key_index(c) < cand, 1.0, 0.0), 0.0)
                return acc + _fold_rows(hit, SUBLANES, jnp.add)

            acc = lax.fori_loop(0, nch, body, jnp.zeros((SUBLANES, qb), F32))
            return jnp.sum(acc, axis=0, keepdims=True).astype(I32)

        jcut = _tie_cutoff(ties_below, need, (1, qb), n_bits)
        jc_scr[...] = jnp.broadcast_to(jnp.where(live, jcut, jnp.int32(-1)), jc_scr.shape)

    jcut = jc_scr[0:1, :]

    def logit_step(g, qg, c):
        s = _dot_nt(key_chunk(kb_ref, c), qg)
        if g == 0:
            m_add = _mask_add(key_scr[c], key_index(c), t, jcut)
            hi_scr[c] = pltpu.bitcast(m_add.astype(jnp.bfloat16), I16)
        else:
            m_add = pltpu.bitcast(hi_scr[c], jnp.bfloat16).astype(F32)
        for i in range(hpg):
            cols = slice(i * qb, (i + 1) * qb)
            si = s[:, cols] + m_add
            s_scr[g][c, :, cols] = si
            mx_scr[g][:, cols] = jnp.maximum(mx_scr[g][:, cols], _fold_rows(si, SUBLANES, jnp.maximum))

    def value_step(g, m, c):
        p = jnp.exp2(s_scr[g][c] - m)
        ls_scr[...] += _fold_rows(p, SUBLANES, jnp.add)
        acc_scr[...] += _dot(vbt_ref[0, c], p.astype(MXU_DTYPE))

    def chunk_loop(step):
        def body(c, carry):
            step(c)
            return carry
        lax.fori_loop(0, nch, body, 0)

    qgs = [jnp.concatenate([qp_ref[0, :, (g * hpg + i) * LANES:(g * hpg + i + 1) * LANES] for i in range(hpg)],
                           axis=0) for g in range(N_KV)]
    for mx in mx_scr:
        mx[...] = jnp.full(mx.shape, NEG_INF, F32)
    chunk_loop(lambda c: logit_step(0, qgs[0], c))
    for g in range(N_KV):
        m = jnp.max(mx_scr[g][...], axis=0, keepdims=True)
        ls_scr[...] = jnp.zeros(ls_scr.shape, F32)
        acc_scr[...] = jnp.zeros(acc_scr.shape, F32)
        if g + 1 < N_KV:
            def both(c, g=g, m=m):
                value_step(g, m, c)
                logit_step(g + 1, qgs[g + 1], c)
            chunk_loop(both)
        else:
            chunk_loop(lambda c, g=g, m=m: value_step(g, m, c))
        o_t = acc_scr[...] / jnp.sum(ls_scr[...], axis=0, keepdims=True)
        feats = slice(g * HEAD_DIM, (g + 1) * HEAD_DIM)
        for pair in range(hpg // 2):
            even = o_t[feats, (2 * pair) * qb:(2 * pair + 1) * qb]
            odd = o_t[feats, (2 * pair + 1) * qb:(2 * pair + 2) * qb]
            tile = g * (hpg // 2) + pair
            o_ref[0, :, tile * LANES:(tile + 1) * LANES] = jnp.concatenate([even, odd], axis=0).T.astype(o_ref.dtype)


def _attn_prompt(qp, qip, wit, ki2, kb, vbt, n_sel):
    b, s, _ = qp.shape
    kc = vbt.shape[3]
    qb = ATT_QUERIES if s % ATT_QUERIES == 0 else Q_BLOCK
    nq = s // qb
    n_bits = max(1, (s - 1).bit_length())
    blk = lambda i, j: (i, j, 0)
    seq = lambda i, j: (i, 0, 0)
    n = (N_HEADS // N_KV) * qb
    return pl.pallas_call(
        functools.partial(_attn_prompt_kernel, n_sel=n_sel, n_bits=n_bits, kc=kc, qb=qb),
        out_shape=jax.ShapeDtypeStruct((b, s, N_HEADS * HEAD_DIM), MXU_DTYPE),
        grid=(b, nq),
        in_specs=[pl.BlockSpec((1, qb, qp.shape[2]), blk),
                  pl.BlockSpec((1, qb, qip.shape[2]), blk),
                  pl.BlockSpec((SUBLANES, qb), lambda i, j: (0, i * nq + j)),
                  pl.BlockSpec((1, s, LANES), seq),
                  pl.BlockSpec((1, s, LANES), seq),
                  pl.BlockSpec((1, s // kc, LANES, kc), lambda i, j: (i, 0, 0, 0))],
        out_specs=pl.BlockSpec((1, qb, N_HEADS * HEAD_DIM), blk),
        scratch_shapes=[pltpu.VMEM((s // kc, kc, qb), I32),
                        pltpu.VMEM((s // kc, kc, qb), I16),
                        pltpu.VMEM((s // kc, kc, qb), I16),
                        pltpu.VMEM((s // kc, kc, n), F32),
                        pltpu.VMEM((s // kc, kc, n), F32),
                        pltpu.VMEM((SUBLANES, n), F32),
                        pltpu.VMEM((SUBLANES, n), F32),
                        pltpu.VMEM((SUBLANES, n), F32),
                        pltpu.VMEM((LANES, n), F32),
                        pltpu.VMEM((SUBLANES, qb), I32)],
        compiler_params=_cparams(2),
        name="attn_prompt",
    )(qp, qip, wit, ki2, kb, vbt)


def _out_kernel(x_ref, yl_ref, ya_ref, mod_ref, g_ref, wo_ref, o_ref, *, lw):
    y = _dot(yl_ref[...], wo_ref[:lw, :]) + _dot(ya_ref[...], wo_ref[lw:, :])
    o_ref[...] = x_ref[...] + mod_ref[0, 5] * _rms(y, g_ref[3:4, :])


def _out_proj(x, y_lru, y_att, mod4, g, wo):
    t, d = x.shape
    lw = y_lru.shape[1]
    tm = _token_tile(t)
    n_tiles = t // tm
    tok = lambda i: (i, 0)
    return pl.pallas_call(
        functools.partial(_out_kernel, lw=lw),
        out_shape=jax.ShapeDtypeStruct((t, d), F32),
        grid=(n_tiles,),
        in_specs=[pl.BlockSpec((tm, d), tok), pl.BlockSpec((tm, lw), tok),
                  pl.BlockSpec((tm, y_att.shape[1]), tok), _mod_spec(mod4, n_tiles),
                  pl.BlockSpec(g.shape, lambda i: (0, 0)), pl.BlockSpec(wo.shape, lambda i: (0, 0))],
        out_specs=pl.BlockSpec((tm, d), tok),
        compiler_params=_cparams(1),
        name="out_proj",
    )(x, y_lru, y_att, mod4, g, wo)


def _attn_sample_kernel(pt_ref, qs_ref, qis_ref, wcol_ref, kin_ref, kn_ref, vn_ref, ckidx_ref, ck_ref, cv_ref,
                        o_ref, kidx_buf, k_buf, v_buf, key_scr, madd_scr, s_scr, sems,
                        *, layer, n_pages, page, ds, n_sel, n_bits, kc):
    bi = pl.program_id(0)
    past = n_pages * page
    total = past + LANES
    n_chunks = past // kc

    streams = ((ckidx_ref, kidx_buf), (ck_ref, k_buf), (cv_ref, v_buf))

    def page_copy(which, p):
        src, dst = streams[which]
        cols = pl.ds(pl.multiple_of(p * page, page), page)
        return pltpu.make_async_copy(src.at[layer, pt_ref[bi, p]], dst.at[:, cols], sems.at[which])

    def for_pages(which, action):
        def body(p, carry):
            action(page_copy(which, p))
            return carry
        lax.fori_loop(0, n_pages, body, 0)

    for which in range(len(streams)):
        for_pages(which, lambda cp: cp.start())

    kidx_buf[:, past:total] = kin_ref[0]
    k_buf[:, past:total] = kn_ref[0]
    v_buf[:, past:total] = vn_ref[0]
    key_scr[...] = jnp.full(key_scr.shape, INT_MIN, I32)

    qis = qis_ref[0]
    wcol = wcol_ref[0][:, 0:1]
    for_pages(0, lambda cp: cp.wait())

    def scores(lo_col, width):
        d = _dot(qis, kidx_buf[:, lo_col:lo_col + width].astype(MXU_DTYPE))
        wgt = jnp.maximum(d, 0.0) * wcol
        sc = wgt[0:ds]
        for h in range(1, IDX_HEADS):
            sc = sc + wgt[h * ds:(h + 1) * ds]
        return sc

    top = SUBLANES // 2
    half = past // 2
    hw = key_scr.shape[1]

    def place(lo_col):
        return (0, lo_col) if lo_col < half else (top, lo_col - half)

    for c in range(n_chunks):
        r0, c0 = place(c * kc)
        key_scr[r0:r0 + ds, c0:c0 + kc] = _score_key(scores(c * kc, kc))
    sc_new = scores(past, LANES)
    lane_new = lax.broadcasted_iota(I32, (ds, LANES), 1)
    row_new = lax.broadcasted_iota(I32, (ds, LANES), 0)
    r0, c0 = place(past)
    key_scr[r0:r0 + ds, c0:c0 + LANES] = jnp.where(lane_new > row_new, jnp.int32(INT_MIN), _score_key(sc_new))

    rows = key_scr.shape[0]
    lane = lax.broadcasted_iota(I32, (rows, LANES), 1)
    col0 = lane + jnp.where(lax.broadcasted_iota(I32, (rows, LANES), 0) >= top, half, 0)

    def count(ind, *row_args):
        args = [jnp.broadcast_to(a, (rows, LANES)) for a in row_args]
        acc = jnp.zeros((rows, LANES), F32)
        for i in range(hw // LANES):
            acc = acc + ind(key_scr[:, i * LANES:(i + 1) * LANES], col0 + i * LANES, *args)
        acc = acc + pltpu.roll(acc, top, 0)
        return jnp.sum(acc, axis=1, keepdims=True)

    t = _kth_largest_key(lambda cand: count(lambda k, col, cb: jnp.where(k >= cb, 1.0, 0.0), cand), (rows, 1), n_sel)
    cnt_gt = count(lambda k, col, tb: jnp.where(k > tb, 1.0, 0.0), t)
    need = n_sel - cnt_gt
    live = t != jnp.int32(INT_MIN)
    jcut = _tie_cutoff(
        lambda cand: count(lambda k, col, tb, cb: jnp.where(k == tb, jnp.where(col < cb, 1.0, 0.0), 0.0), t, cand),
        need, (rows, 1), n_bits)
    jcut = jnp.where(live, jcut, jnp.int32(-1))
    tb, jb = jnp.broadcast_to(t, (rows, LANES)), jnp.broadcast_to(jcut, (rows, LANES))
    for i in range(hw // LANES):
        cols = slice(i * LANES, (i + 1) * LANES)
        madd_scr[:, cols] = _mask_add(key_scr[:, cols], col0 + i * LANES, tb, jb)

    qs = qs_ref[0]
    nrow = qs.shape[0]

    def row_mask(lo_col, width):
        r0, c0 = place(lo_col)
        return jnp.concatenate(
            [jnp.broadcast_to(madd_scr[r0 + tt:r0 + tt + 1, c0:c0 + width], (N_HEADS, width)) for tt in range(ds)],
            axis=0)

    spans = [(c * kc, kc) for c in range(n_chunks)] + [(past, LANES)]
    for_pages(1, lambda cp: cp.wait())
    m = jnp.full((nrow, 1), NEG_INF, F32)
    for lo_col, width in spans:
        s = _dot(qs, k_buf[:, lo_col:lo_col + width].astype(MXU_DTYPE)) + row_mask(lo_col, width)
        s_scr[:, lo_col:lo_col + width] = s
        m = jnp.maximum(m, jnp.max(s, axis=1, keepdims=True))
    for_pages(2, lambda cp: cp.wait())
    l = jnp.zeros((nrow, 1), F32)
    acc = jnp.zeros((nrow, LANES), F32)
    for lo_col, width in spans:
        p = jnp.exp2(s_scr[:, lo_col:lo_col + width] - m)
        l = l + jnp.sum(p, axis=1, keepdims=True)
        acc = acc + _dot_nt(p.astype(MXU_DTYPE), v_buf[:, lo_col:lo_col + width].astype(MXU_DTYPE))
    o_ref[0] = acc / l


def _attn_sample(page_table, qs, qis, wcol, ki_new, k_new, v_new, cache_kidx, cache_k2, cache_v2, layer, n_sel, ds):
    db, n_pages = page_table.shape
    page = cache_kidx.shape[3]
    past = n_pages * page
    total = past + LANES
    kc = 2048 if past % 4096 == 0 else page
    assert (past // kc) % 2 == 0 and 2 * ds <= SUBLANES
    n_bits = max(1, (total - 1).bit_length())
    per = lambda i, pt: (i, 0, 0)
    any_spec = pl.BlockSpec(memory_space=pl.ANY)
    grid_spec = pltpu.PrefetchScalarGridSpec(
        num_scalar_prefetch=1,
        grid=(db,),
        in_specs=[pl.BlockSpec((1,) + qs.shape[1:], per), pl.BlockSpec((1,) + qis.shape[1:], per),
                  pl.BlockSpec((1,) + wcol.shape[1:], per), pl.BlockSpec((1,) + ki_new.shape[1:], per),
                  pl.BlockSpec((1,) + k_new.shape[1:], per), pl.BlockSpec((1,) + v_new.shape[1:], per),
                  any_spec, any_spec, any_spec],
        out_specs=pl.BlockSpec((1, qs.shape[1], LANES), per),
        scratch_shapes=[pltpu.VMEM((IDX_DIM, total), F32),
                        pltpu.VMEM((LANES, total), F32),
                        pltpu.VMEM((LANES, total), F32),
                        pltpu.VMEM((SUBLANES, past // 2 + LANES), I32),
                        pltpu.VMEM((SUBLANES, past // 2 + LANES), F32),
                        pltpu.VMEM((qs.shape[1], total), F32),
                        pltpu.SemaphoreType.DMA((3,))])
    return pl.pallas_call(
        functools.partial(_attn_sample_kernel, layer=layer, n_pages=n_pages, page=page, ds=ds, n_sel=n_sel,
                          n_bits=n_bits, kc=kc),
        out_shape=jax.ShapeDtypeStruct((db, qs.shape[1], LANES), F32),
        grid_spec=grid_spec,
        compiler_params=_cparams(1),
        name="attn_sample",
    )(page_table, qs, qis, wcol, ki_new, k_new, v_new, cache_kidx, cache_k2, cache_v2)


def _rope_tables(pos):
    half = ROPE_DIM // 2
    freqs = ROPE_THETA ** (-(jnp.arange(half, dtype=F32) / half))
    ang = pos.astype(F32)[:, None] * freqs[None, :]
    cos, sin = jnp.cos(ang), jnp.sin(ang)
    rest = HEAD_DIM - ROPE_DIM
    c64 = jnp.concatenate([cos, cos, jnp.ones((pos.shape[0], rest), F32)], axis=1)
    s64 = jnp.concatenate([-sin, sin, jnp.zeros((pos.shape[0], rest), F32)], axis=1)
    return jnp.tile(c64, (1, LANES // HEAD_DIM)), jnp.tile(s64, (1, LANES // HEAD_DIM))


def _block_diag(w):
    nb, n, _ = w.shape
    eye = jnp.eye(nb, dtype=w.dtype)
    return (eye[:, None, :, None] * w[:, :, None, :]).reshape(nb * n, nb * n)


def kernel(x_prompt, x_sample, c_prompt, c_sample, cache_k, cache_v, cache_kidx, state_h, state_conv, page_table,
           w_ada, b_ada, norm_g, ffn_w_gu, ffn_w_down, w_in, conv_w, conv_b, lru_wa, lru_ba, lru_wx, lru_bx,
           lru_lambda, w_out):
    b, s, d = x_prompt.shape
    db, ds, _ = x_sample.shape
    depth = w_ada.shape[0]
    lw = lru_lambda.shape[1]
    n_pool, page = cache_k.shape[1], cache_k.shape[2]
    past = page_table.shape[1] * page
    kv_w = N_KV * HEAD_DIM
    tm_p = _token_tile(b * s)
    assert kv_w == LANES and 2 * IDX_DIM == LANES and s % tm_p == 0 and tm_p % Q_BLOCK == 0 and page % LANES == 0

    cos_p, sin_p = _rope_tables(jnp.arange(s, dtype=I32))
    cos_s, sin_s = _rope_tables(past + jnp.arange(ds, dtype=I32))
    cos_s, sin_s = jnp.tile(cos_s, (db, 1)), jnp.tile(sin_s, (db, 1))

    n_c = b + db
    c_all = jnp.concatenate([c_prompt, c_sample, jnp.zeros((-n_c % SUBLANES, d), F32)], axis=0)
    cache_kidx_t = cache_kidx.transpose(0, 1, 3, 2)
    cache_k_t = cache_k.transpose(0, 1, 3, 4, 2).reshape(depth, n_pool, kv_w, page)
    cache_v_t = cache_v.transpose(0, 1, 3, 4, 2).reshape(depth, n_pool, kv_w, page)

    def new_cols(a):
        a = a.reshape(db, ds, a.shape[-1]).transpose(0, 2, 1)
        return jnp.pad(a, ((0, 0), (0, 0), (0, LANES - ds)))

    xp = x_prompt.reshape(b * s, d)
    xs = x_sample.reshape(db * ds, d)
    n_sel_p = min(TOP_K, s // 4)
    n_sel_s = min(TOP_K, (past + ds) // 4)
    q_w = N_HEADS * HEAD_DIM
    qi_w = IDX_HEADS * IDX_DIM
    col_ki = 2 * lw + q_w + 2 * kv_w + qi_w

    outs = {name: [] for name in ("kp", "vp", "kip", "hp", "cp", "ks", "vs", "kis", "hs", "cs")}
    for l in range(depth):
        mod = _ada_mod(c_all, w_ada[l], b_ada[l])
        mod_p = mod[:b].reshape(b, N_MOD, 1, d)
        mod_s = jnp.repeat(mod[b:n_c].reshape(db, N_MOD, d), ds, axis=0).transpose(1, 0, 2)[None]
        g = norm_g[l]
        wgu = ffn_w_gu[l].astype(MXU_DTYPE)
        wdn = ffn_w_down[l].astype(MXU_DTYPE)
        w_ki = w_in[l][:, col_ki:col_ki + IDX_DIM]
        w_pad = jnp.concatenate(
            [w_in[l][:, :col_ki], w_ki, w_ki, w_in[l][:, col_ki + IDX_DIM:],
             jnp.zeros((d, LANES - IDX_HEADS), F32)], axis=1).astype(MXU_DTYPE)
        wo = w_out[l].astype(MXU_DTYPE)
        wa_bd = _block_diag(lru_wa[l]).astype(MXU_DTYPE)
        wx_bd = _block_diag(lru_wx[l]).astype(MXU_DTYPE)
        lru_args = (conv_w[l], conv_b[l].reshape(1, lw), wa_bd, lru_ba[l].reshape(1, lw), wx_bd,
                    lru_bx[l].reshape(1, lw), lru_lambda[l].reshape(1, lw))

        xp = _ffn(xp, mod_p, g, wgu[0], wdn[0], 0, 0)
        xl, gl, qp, k, v, kb, vbt, qip, ki, ki2, _, wit = _proj(xp, mod_p, g, w_pad, cos_p, sin_p, lw, s // tm_p)
        r3 = lambda a: a.reshape(b, s, a.shape[-1])
        y_lru, h_t, nbuf = _lru_prompt(r3(xl), r3(gl), jnp.zeros((b, CONV_W - 1, lw), F32), jnp.zeros((b, lw), F32),
                                       *lru_args)
        y_att = _attn_prompt(r3(qp), r3(qip), wit, r3(ki2), r3(kb), vbt.reshape(b, s // tm_p, LANES, tm_p), n_sel_p)
        xp = _out_proj(xp, y_lru.reshape(b * s, lw), y_att.reshape(b * s, q_w), mod_p, g, wo)
        xp = _ffn(xp, mod_p, g, wgu[1], wdn[1], 6, 4)
        outs["kp"].append(k.reshape(b, s, N_KV, HEAD_DIM))
        outs["vp"].append(v.reshape(b, s, N_KV, HEAD_DIM))
        outs["kip"].append(ki.reshape(b, s, IDX_DIM))
        outs["hp"].append(h_t.reshape(b, lw))
        outs["cp"].append(nbuf)

        xs = _ffn(xs, mod_s, g, wgu[0], wdn[0], 0, 0)
        xl, gl, qp, k, v, _, _, qip, ki, _, wi, _ = _proj(xs, mod_s, g, w_pad, cos_s, sin_s, lw, 1)
        tmaj = lambda a: a.reshape(db, ds, a.shape[-1]).transpose(1, 0, 2)
        y_lru, h_t, nbuf = _lru_sample(tmaj(xl), tmaj(gl), state_conv[l].transpose(1, 0, 2), state_h[l], *lru_args)
        qs = qp.reshape(db, ds * N_HEADS, LANES)
        qis = qip.reshape(db, ds, IDX_HEADS, 2, IDX_DIM)
        qis = jnp.stack([qis[:, :, h, h % 2] for h in range(IDX_HEADS)], axis=1).reshape(db, IDX_HEADS * ds, IDX_DIM)
        wcol = wi.reshape(db, ds, LANES)[:, :, :IDX_HEADS].transpose(0, 2, 1).reshape(db, IDX_HEADS * ds, 1)
        wcol = jnp.broadcast_to(wcol, (db, IDX_HEADS * ds, LANES))
        o_s = _attn_sample(page_table, qs, qis, wcol, new_cols(ki), new_cols(k), new_cols(v),
                           cache_kidx_t, cache_k_t, cache_v_t, l, n_sel_s, ds)
        o_s = o_s.reshape(db, ds, N_HEADS, N_KV, HEAD_DIM)
        hpg = N_HEADS // N_KV
        y_att = jnp.stack([o_s[:, :, h, h // hpg] for h in range(N_HEADS)], axis=2).reshape(db * ds, q_w)
        xs = _out_proj(xs, y_lru.transpose(1, 0, 2).reshape(db * ds, lw), y_att.astype(MXU_DTYPE), mod_s, g, wo)
        xs = _ffn(xs, mod_s, g, wgu[1], wdn[1], 6, 4)
        outs["ks"].append(k.reshape(db, ds, N_KV, HEAD_DIM))
        outs["vs"].append(v.reshape(db, ds, N_KV, HEAD_DIM))
        outs["kis"].append(ki.reshape(db, ds, IDX_DIM))
        outs["hs"].append(h_t)
        outs["cs"].append(nbuf.transpose(1, 0, 2))

    st = lambda name: jnp.stack(outs[name])
    return (xp.reshape(b, s, d), xs.reshape(db, ds, d), st("kp"), st("vp"), st("kip"), st("hp"), st("cp"),
            st("ks"), st("vs"), st("kis"), st("hs"), st("cs"))
```

```python
import functools
import math

import jax
import jax.numpy as jnp
from jax import lax
from jax.experimental import pallas as pl
from jax.experimental.pallas import tpu as pltpu

F32 = jnp.float32
I32 = jnp.int32
I16 = jnp.int16
MXU_DTYPE = jnp.bfloat16

LRU_BLOCKS = 8
CONV_W = 4
LRU_C = 8.0
N_HEADS = 8
HEAD_DIM = 64
N_KV = 2
ROPE_DIM = HEAD_DIM // 4
ROPE_THETA = 500000.0
IDX_HEADS = 4
IDX_DIM = 64
TOP_K = 256
Q_BLOCK = 128
N_MOD = 9
RMS_EPS = 1e-6
LOG2E = math.log2(math.e)

LANES = 128
SUBLANES = 8
PACKED_SUBLANES = 16
TOKEN_TILE = 512
ATT_QUERIES = 256
VMEM_LIMIT = 56 * 1024 * 1024
INT_MIN = -2147483648
HALF_RANGE = 32768
NEG_INF = float("-inf")


def _cparams(n_axes):
    return pltpu.CompilerParams(dimension_semantics=("arbitrary",) * n_axes,
                                vmem_limit_bytes=VMEM_LIMIT)


def _rms(x, g):
    ms = jnp.mean(x * x, axis=-1, keepdims=True)
    return x * lax.rsqrt(ms + RMS_EPS) * g


def _dot(a, b):
    return jnp.dot(a, b, preferred_element_type=F32)


def _dot_nt(a, b):
    return lax.dot_general(a, b, (((1,), (1,)), ((), ())), preferred_element_type=F32)


def _mod_kernel(c_ref, w_ref, b_ref, o_ref):
    o_ref[...] = jnp.dot(c_ref[...], w_ref[...], preferred_element_type=F32,
                         precision=lax.Precision.HIGHEST) + b_ref[...]


def _ada_mod(c_all, w, b):
    rows, d = c_all.shape
    n = w.shape[1]
    tn = d
    return pl.pallas_call(
        _mod_kernel,
        out_shape=jax.ShapeDtypeStruct((rows, n), F32),
        grid=(n // tn,),
        in_specs=[pl.BlockSpec((rows, d), lambda i: (0, 0)),
                  pl.BlockSpec((d, tn), lambda i: (0, i)),
                  pl.BlockSpec((1, tn), lambda i: (0, i))],
        out_specs=pl.BlockSpec((rows, tn), lambda i: (0, i)),
        compiler_params=_cparams(1),
        name="ada_mod",
    )(c_all, w, b.reshape(1, n))


def _ffn_kernel(x_ref, mod_ref, g_ref, wgu_ref, wdn_ref, o_ref, *, m0, gi, d_ff):
    x = x_ref[...]
    sh, sc, gt = mod_ref[0, m0], mod_ref[0, m0 + 1], mod_ref[0, m0 + 2]
    h = (_rms(x, g_ref[gi:gi + 1, :]) * (1.0 + sc) + sh).astype(MXU_DTYPE)
    g = _dot(h, wgu_ref[:, :d_ff])
    u = _dot(h, wgu_ref[:, d_ff:])
    a = (g * jax.nn.sigmoid(g) * u).astype(MXU_DTYPE)
    f = _dot(a, wdn_ref[...])
    o_ref[...] = x + 0.5 * gt * _rms(f, g_ref[gi + 1:gi + 2, :])


def _token_tile(t):
    return TOKEN_TILE if t % TOKEN_TILE == 0 else t


def _mod_spec(mod4, n_tiles):
    nb, _, r, d = mod4.shape
    per = n_tiles // nb
    return pl.BlockSpec((1, N_MOD, r, d), lambda i: (i // per, 0, 0, 0))


def _ffn(x, mod4, g, wgu, wdn, m0, gi):
    t, d = x.shape
    d_ff = wdn.shape[0]
    tm = _token_tile(t)
    n_tiles = t // tm
    return pl.pallas_call(
        functools.partial(_ffn_kernel, m0=m0, gi=gi, d_ff=d_ff),
        out_shape=jax.ShapeDtypeStruct((t, d), F32),
        grid=(n_tiles,),
        in_specs=[pl.BlockSpec((tm, d), lambda i: (i, 0)),
                  _mod_spec(mod4, n_tiles),
                  pl.BlockSpec(g.shape, lambda i: (0, 0)),
                  pl.BlockSpec(wgu.shape, lambda i: (0, 0)),
                  pl.BlockSpec(wdn.shape, lambda i: (0, 0))],
        out_specs=pl.BlockSpec((tm, d), lambda i: (i, 0)),
        compiler_params=_cparams(1),
        name="ffn",
    )(x, mod4, g, wgu, wdn)


def _rope(x, cos, sin):
    lane = lax.broadcasted_iota(I32, x.shape, 1)
    first = (lane & (HEAD_DIM - 1)) < (ROPE_DIM // 2)
    partner = jnp.where(first, pltpu.roll(x, LANES - ROPE_DIM // 2, 1), pltpu.roll(x, ROPE_DIM // 2, 1))
    return x * cos + partner * sin


def _proj_kernel(x_ref, mod_ref, g_ref, w_ref, cos_ref, sin_ref,
                 xl_ref, gl_ref, qp_ref, k_ref, v_ref, kb_ref, vbt_ref, qip_ref, ki_ref, ki2_ref, wi_ref, wit_ref,
                 *, lw):
    x = x_ref[...]
    h = (_rms(x, g_ref[2:3, :]) * (1.0 + mod_ref[0, 4]) + mod_ref[0, 3]).astype(MXU_DTYPE)
    p = _dot(h, w_ref[...])
    cos, sin = cos_ref[...], sin_ref[...]
    lane = lax.broadcasted_iota(I32, cos.shape, 1)
    lo = lane < HEAD_DIM

    o = 0
    xl_ref[...] = p[:, o:o + lw]
    o += lw
    gl_ref[...] = p[:, o:o + lw]
    o += lw
    q_scale = HEAD_DIM ** -0.5 * LOG2E
    for pair in range(N_HEADS // 2):
        qr = _rope(p[:, o + pair * LANES:o + (pair + 1) * LANES], cos, sin) * q_scale
        qr_sw = pltpu.roll(qr, HEAD_DIM, 1)
        for half in range(2):
            hd = 2 * pair + half
            grp = hd // (N_HEADS // N_KV)
            src = qr if half == grp else qr_sw
            slab = jnp.where(lo if grp == 0 else jnp.logical_not(lo), src, 0.0)
            qp_ref[:, hd * LANES:(hd + 1) * LANES] = slab.astype(qp_ref.dtype)
    o += N_HEADS * HEAD_DIM
    k = _rope(p[:, o:o + LANES], cos, sin)
    k_ref[...] = k
    kb_ref[...] = k.astype(kb_ref.dtype)
    o += LANES
    v = p[:, o:o + LANES]
    v_ref[...] = v
    vbt_ref[0] = v.T.astype(vbt_ref.dtype)
    o += LANES
    for pair in range(IDX_HEADS // 2):
        qi = _rope(p[:, o + pair * LANES:o + (pair + 1) * LANES], cos, sin)
        for half in range(2):
            hd = 2 * pair + half
            slab = jnp.where(lo if half == 0 else jnp.logical_not(lo), qi, 0.0)
            qip_ref[:, hd * LANES:(hd + 1) * LANES] = slab.astype(qip_ref.dtype)
    o += IDX_HEADS * IDX_DIM
    ki2 = _rope(p[:, o:o + LANES], cos, sin)
    ki_ref[...] = ki2[:, :IDX_DIM]
    ki2_ref[...] = ki2.astype(ki2_ref.dtype)
    o += LANES
    wi = p[:, o:o + LANES] * (IDX_DIM ** -0.5 * IDX_HEADS ** -0.5)
    wi_ref[...] = wi
    wit_ref[...] = wi.T[:SUBLANES, :]


def _proj(x, mod4, g, w_pad, cos, sin, lw, pos_tiles):
    t, d = x.shape
    tm = _token_tile(t)
    n_tiles = t // tm
    tok = lambda i: (i, 0)
    pos = lambda i: (i % pos_tiles, 0)
    row_major = [(lw, F32), (lw, F32), (N_HEADS * LANES, MXU_DTYPE), (LANES, F32), (LANES, F32),
                 (LANES, MXU_DTYPE), None, (IDX_HEADS * LANES, MXU_DTYPE),
                 (IDX_DIM, F32), (LANES, MXU_DTYPE), (LANES, F32), None]
    out_shape, out_specs = [], []
    for idx, entry in enumerate(row_major):
        if entry is not None:
            out_shape.append(jax.ShapeDtypeStruct((t, entry[0]), entry[1]))
            out_specs.append(pl.BlockSpec((tm, entry[0]), tok))
        elif idx == 6:
            out_shape.append(jax.ShapeDtypeStruct((n_tiles, LANES, tm), MXU_DTYPE))
            out_specs.append(pl.BlockSpec((1, LANES, tm), lambda i: (i, 0, 0)))
        else:
            out_shape.append(jax.ShapeDtypeStruct((SUBLANES, t), F32))
            out_specs.append(pl.BlockSpec((SUBLANES, tm), lambda i: (0, i)))
    return pl.pallas_call(
        functools.partial(_proj_kernel, lw=lw),
        out_shape=out_shape,
        grid=(n_tiles,),
        in_specs=[pl.BlockSpec((tm, d), tok),
                  _mod_spec(mod4, n_tiles),
                  pl.BlockSpec(g.shape, lambda i: (0, 0)),
                  pl.BlockSpec(w_pad.shape, lambda i: (0, 0)),
                  pl.BlockSpec((tm, LANES), pos),
                  pl.BlockSpec((tm, LANES), pos)],
        out_specs=out_specs,
        compiler_params=_cparams(1),
        name="mixer_proj",
    )(x, mod4, g, w_pad, cos, sin)


def _gelu_tanh(x):
    return x * (0.5 * (1.0 + jnp.tanh(0.7978845608028654 * (x + 0.044715 * (x * x * x)))))


def _lru_coeffs(xc, wa_ref, ba_ref, wx_ref, bx_ref, lam_ref):
    xcb = xc.astype(MXU_DTYPE)
    r = jax.nn.sigmoid(_dot(xcb, wa_ref[...]) + ba_ref[...])
    i = jax.nn.sigmoid(_dot(xcb, wx_ref[...]) + bx_ref[...])
    nl = -lam_ref[...]
    softplus = jnp.maximum(nl, 0.0) + jnp.log1p(jnp.exp(-jnp.abs(nl)))
    log_a = (-LRU_C) * r * softplus
    a = jnp.exp(log_a)
    one_minus_a2 = -jnp.tanh(log_a) * (a * a + 1.0)
    return a, jnp.sqrt(one_minus_a2) * (i * xc)


def _lru_prompt_kernel(xl_ref, gl_ref, buf_ref, h0_ref, cw_ref, cb_ref, wa_ref, ba_ref, wx_ref, bx_ref, lam_ref,
                       y_ref, ht_ref, nb_ref, xs_scr, a_scr, b_scr, h_scr, *, ts):
    tj = pl.program_id(1)
    pad = SUBLANES
    ng = ts // SUBLANES

    @pl.when(tj == 0)
    def _():
        xs_scr[0:pad, :] = jnp.zeros((pad, xs_scr.shape[1]), F32)
        xs_scr[pad - (CONV_W - 1):pad, :] = buf_ref[0]
        h_scr[...] = h0_ref[0]

    x = xl_ref[0]
    xs_scr[pad:pad + ts, :] = x
    xc = cb_ref[...] + x * cw_ref[CONV_W - 1:CONV_W, :]
    for j in range(1, CONV_W):
        xc = xc + xs_scr[pad - j:pad - j + ts, :] * cw_ref[CONV_W - 1 - j:CONV_W - j, :]

    a, b = _lru_coeffs(xc, wa_ref, ba_ref, wx_ref, bx_ref, lam_ref)
    lw = a.shape[1]
    a = a.reshape(ng, SUBLANES, lw)
    b = b.reshape(ng, SUBLANES, lw)
    sub = lax.broadcasted_iota(I32, (ng, SUBLANES, lw), 1)
    d = 1
    while d < SUBLANES:
        keep = sub >= d
        a_sh = jnp.where(keep, pltpu.roll(a, d, 1), 1.0)
        b_sh = jnp.where(keep, pltpu.roll(b, d, 1), 0.0)
        b = a * b_sh + b
        a = a * a_sh
        d *= 2
    a_scr[...] = a
    b_scr[...] = b

    def carry(gidx, h):
        hg = a_scr[gidx] * h + b_scr[gidx]
        b_scr[gidx] = hg
        return hg[SUBLANES - 1:SUBLANES, :]

    h_last = lax.fori_loop(0, ng, carry, h_scr[...], unroll=8)
    h_scr[...] = h_last
    hs = b_scr[...].reshape(ts, lw)
    y_ref[0] = (hs * _gelu_tanh(gl_ref[0])).astype(y_ref.dtype)
    xs_scr[0:pad, :] = xs_scr[ts:ts + pad, :]

    @pl.when(tj == pl.num_programs(1) - 1)
    def _():
        ht_ref[0] = h_last
        nb_ref[0] = xs_scr[pad - (CONV_W - 1):pad, :]


def _lru_prompt(xl, gl, buf0, h0, cw, cb, wa_bd, ba, wx_bd, bx, lam):
    b, s, lw = xl.shape
    ts = _token_tile(s)
    seq = lambda i, j: (i, j, 0)
    per_b = lambda i, j: (i, 0, 0)
    const = lambda i, j: (0, 0)
    row = pl.BlockSpec((1, lw), const)
    return pl.pallas_call(
        functools.partial(_lru_prompt_kernel, ts=ts),
        out_shape=[jax.ShapeDtypeStruct((b, s, lw), MXU_DTYPE),
                   jax.ShapeDtypeStruct((b, 1, lw), F32),
                   jax.ShapeDtypeStruct((b, CONV_W - 1, lw), F32)],
        grid=(b, s // ts),
        in_specs=[pl.BlockSpec((1, ts, lw), seq), pl.BlockSpec((1, ts, lw), seq),
                  pl.BlockSpec((1, CONV_W - 1, lw), per_b), pl.BlockSpec((1, 1, lw), per_b),
                  pl.BlockSpec((CONV_W, lw), const), row,
                  pl.BlockSpec((lw, lw), const), row, pl.BlockSpec((lw, lw), const), row, row],
        out_specs=[pl.BlockSpec((1, ts, lw), seq), pl.BlockSpec((1, 1, lw), per_b),
                   pl.BlockSpec((1, CONV_W - 1, lw), per_b)],
        scratch_shapes=[pltpu.VMEM((ts + SUBLANES, lw), F32),
                        pltpu.VMEM((ts // SUBLANES, SUBLANES, lw), F32),
                        pltpu.VMEM((ts // SUBLANES, SUBLANES, lw), F32),
                        pltpu.VMEM((1, lw), F32)],
        compiler_params=_cparams(2),
        name="rglru_prompt",
    )(xl, gl, buf0, h0.reshape(b, 1, lw), cw, cb, wa_bd, ba, wx_bd, bx, lam)


def _lru_sample_kernel(xl_ref, gl_ref, buf_ref, h0_ref, cw_ref, cb_ref, wa_ref, ba_ref, wx_ref, bx_ref, lam_ref,
                       y_ref, ht_ref, nb_ref, *, ds):
    rows = [buf_ref[j] for j in range(CONV_W - 1)] + [xl_ref[t] for t in range(ds)]
    xcs = []
    for t in range(ds):
        xc = cb_ref[...]
        for j in range(CONV_W):
            xc = xc + rows[t + j] * cw_ref[j:j + 1, :]
        xcs.append(xc)
    db = xcs[0].shape[0]
    a, b = _lru_coeffs(jnp.concatenate(xcs, axis=0), wa_ref, ba_ref, wx_ref, bx_ref, lam_ref)
    h = h0_ref[...]
    for t in range(ds):
        h = a[t * db:(t + 1) * db] * h + b[t * db:(t + 1) * db]
        y_ref[t] = (h * _gelu_tanh(gl_ref[t])).astype(y_ref.dtype)
    ht_ref[...] = h
    for j in range(CONV_W - 1):
        nb_ref[j] = rows[ds + j]


def _lru_sample(xl_t, gl_t, buf_t, h0, cw, cb, wa_bd, ba, wx_bd, bx, lam):
    ds, db, lw = xl_t.shape
    full = lambda a: pl.BlockSpec(a.shape, lambda i, n=a.ndim: (0,) * n)
    args = (xl_t, gl_t, buf_t, h0, cw, cb, wa_bd, ba, wx_bd, bx, lam)
    outs = [jax.ShapeDtypeStruct((ds, db, lw), MXU_DTYPE), jax.ShapeDtypeStruct((db, lw), F32),
            jax.ShapeDtypeStruct((CONV_W - 1, db, lw), F32)]
    return pl.pallas_call(
        functools.partial(_lru_sample_kernel, ds=ds),
        out_shape=outs,
        grid=(1,),
        in_specs=[full(a) for a in args],
        out_specs=[full(o) for o in outs],
        compiler_params=_cparams(1),
        name="rglru_sample",
    )(*args)


def _score_key(sc):
    bits = lax.bitcast_convert_type(sc, I32)
    key = jnp.where(bits < 0, bits ^ jnp.int32(0x7FFFFFFF), bits)
    return jnp.where(sc == 0.0, 0, key)


def _kth_largest_key(count_ge, shape, n_sel):
    def bit_body(i, t_u):
        cand_u = t_u | jnp.left_shift(jnp.int32(1), 31 - i)
        cnt = count_ge(cand_u ^ jnp.int32(INT_MIN))
        return jnp.where(cnt >= n_sel, cand_u, t_u)

    t_u = lax.fori_loop(0, 32, bit_body, jnp.zeros(shape, I32))
    return t_u ^ jnp.int32(INT_MIN)


def _kth_largest_key_radix4(count_ge3, shape, n_sel):
    int_min = jnp.int32(INT_MIN)

    def step(i, t_u):
        sh = 30 - 2 * i
        c1, c2, c3 = (t_u | jnp.left_shift(jnp.int32(d), sh) for d in (1, 2, 3))
        n1, n2, n3 = count_ge3(c1 ^ int_min, c2 ^ int_min, c3 ^ int_min)
        return jnp.where(n3 >= n_sel, c3, jnp.where(n2 >= n_sel, c2, jnp.where(n1 >= n_sel, c1, t_u)))

    t_u = lax.fori_loop(0, 16, step, jnp.zeros(shape, I32))
    return t_u ^ int_min


def _tie_cutoff(count_tie_below, need, shape, n_bits):
    def bit_body(i, j_u):
        cand = j_u | jnp.left_shift(jnp.int32(1), n_bits - 1 - i)
        return jnp.where(count_tie_below(cand) < need, cand, j_u)

    return lax.fori_loop(0, n_bits, bit_body, jnp.zeros(shape, I32))


def _mask_add(key, idx, t, jcut):
    tie = jnp.where(idx <= jcut, 0.0, NEG_INF)
    return jnp.where(key > t, 0.0, jnp.where(key == t, tie, NEG_INF))


def _fold_rows(x, group, op):
    slabs = [x[i * group:(i + 1) * group] for i in range(x.shape[0] // group)]
    chains = slabs[:4]
    for i, slab in enumerate(slabs[4:]):
        chains[i % len(chains)] = op(chains[i % len(chains)], slab)
    while len(chains) > 1:
        chains = [op(chains[i], chains[i + 1]) if i + 1 < len(chains) else chains[i]
                  for i in range(0, len(chains), 2)]
    return chains[0]


def _attn_prompt_kernel(qp_ref, qip_ref, wit_ref, ki2_ref, kb_ref, vbt_ref, o_ref,
                        key_scr, hi_scr, lo_scr, s0_scr, s1_scr, mx0_scr, mx1_scr, ls_scr, acc_scr, jc_scr,
                        *, n_sel, n_bits, kc, qb):
    s_scr, mx_scr = (s0_scr, s1_scr), (mx0_scr, mx1_scr)
    jb = pl.program_id(1)
    nch = ((jb + 1) * qb - 1) // kc + 1
    hpg = N_HEADS // N_KV
    n = hpg * qb

    def key_chunk(ref, c):
        return ref[0, pl.ds(pl.multiple_of(c * kc, kc), kc), :]

    def key_index(c):
        return c * kc + lax.broadcasted_iota(I32, (kc, qb), 0)

    qpos = jb * qb + lax.broadcasted_iota(I32, (kc, qb), 1)

    qi_all = qip_ref[0]
    qi_stack = jnp.concatenate([qi_all[:, h * LANES:(h + 1) * LANES] for h in range(IDX_HEADS)], axis=0)
    wt = wit_ref[...]

    def score_body(c, carry):
        d = _dot_nt(key_chunk(ki2_ref, c), qi_stack)
        sc = jnp.maximum(d[:, 0:qb], 0.0) * wt[0:1, :]
        for h in range(1, IDX_HEADS):
            sc = sc + jnp.maximum(d[:, h * qb:(h + 1) * qb], 0.0) * wt[h:h + 1, :]
        key = jnp.where(key_index(c) > qpos, jnp.int32(INT_MIN), _score_key(sc))
        key_scr[c] = key
        hi_scr[c] = lax.shift_right_arithmetic(key, 16).astype(I16)
        lo_scr[c] = ((key & 0xFFFF) - HALF_RANGE).astype(I16)
        return carry

    lax.fori_loop(0, nch, score_body, 0)

    def rows16(x):
        return jnp.broadcast_to(x, (PACKED_SUBLANES, qb)).astype(I16)

    def count16(src, ind, *args):
        def body(c, accs):
            v = src[c]
            accs = list(accs)
            for i in range(kc // PACKED_SUBLANES):
                j = i % len(accs)
                accs[j] = accs[j] + ind(v[i * PACKED_SUBLANES:(i + 1) * PACKED_SUBLANES], *args)
            return tuple(accs)

        zero = jnp.zeros((PACKED_SUBLANES, qb), I16)
        accs = lax.fori_loop(0, nch, body, (zero, zero, zero, zero))
        acc = (accs[0] + accs[1]) + (accs[2] + accs[3])
        return jnp.sum(acc.astype(I32), axis=0, keepdims=True)

    one16, zero16 = jnp.int16(1), jnp.int16(0)

    def digit_search(src, need):
        def bit_body(i, d_u):
            cand_u = d_u | jnp.left_shift(jnp.int32(1), 15 - i)
            cnt = count16(src, lambda v, c: jnp.where(v >= c, one16, zero16), rows16(cand_u - HALF_RANGE))
            return jnp.where(cnt >= need, cand_u, d_u)

        return lax.fori_loop(0, 16, bit_body, jnp.zeros((1, qb), I32)) - HALF_RANGE

    t_hi = digit_search(hi_scr, n_sel)
    t_hi16 = rows16(t_hi)
    cnt_hi_gt = count16(hi_scr, lambda v, c: jnp.where(v > c, one16, zero16), t_hi16)

    def low_digit_body(c, carry):
        hi, lo = hi_scr[c], lo_scr[c]
        out = []
        for i in range(kc // PACKED_SUBLANES):
            rows = slice(i * PACKED_SUBLANES, (i + 1) * PACKED_SUBLANES)
            out.append(jnp.where(hi[rows] == t_hi16, lo[rows], jnp.int16(-HALF_RANGE)))
        lo_scr[c] = jnp.concatenate(out, axis=0)
        return carry

    lax.fori_loop(0, nch, low_digit_body, 0)
    t_lo = digit_search(lo_scr, n_sel - cnt_hi_gt)
    t_lo16 = rows16(t_lo)
    cnt_gt = cnt_hi_gt + count16(lo_scr, lambda v, c: jnp.where(v > c, one16, zero16), t_lo16)
    cnt_ge = cnt_hi_gt + count16(lo_scr, lambda v, c: jnp.where(v >= c, one16, zero16), t_lo16)
    t = t_hi * (2 * HALF_RANGE) + (t_lo + HALF_RANGE)
    need = n_sel - cnt_gt
    live = t != jnp.int32(INT_MIN)
    excess = jnp.where(jnp.logical_and(live, cnt_ge > n_sel), 1.0, 0.0)
    jc_scr[...] = jnp.broadcast_to(jnp.where(live, jnp.int32(2 ** 30), jnp.int32(-1)), jc_scr.shape)

    @pl.when(jnp.max(excess) > 0.0)
    def _():
        def ties_below(cand):
            def body(c, acc):
                hit = jnp.where(key_scr[c] == t, jnp.where(key_index(c) < cand, 1.0, 0.0), 0.0)
                return acc + _fold_rows(hit, SUBLANES, jnp.add)

            acc = lax.fori_loop(0, nch, body, jnp.zeros((SUBLANES, qb), F32))
            return jnp.sum(acc, axis=0, keepdims=True).astype(I32)

        jcut = _tie_cutoff(ties_below, need, (1, qb), n_bits)
        jc_scr[...] = jnp.broadcast_to(jnp.where(live, jcut, jnp.int32(-1)), jc_scr.shape)

    jcut = jc_scr[0:1, :]

    def logit_step(g, qg, c):
        s = _dot_nt(key_chunk(kb_ref, c), qg)
        if g == 0:
            m_add = _mask_add(key_scr[c], key_index(c), t, jcut)
            hi_scr[c] = pltpu.bitcast(m_add.astype(jnp.bfloat16), I16)
        else:
            m_add = pltpu.bitcast(hi_scr[c], jnp.bfloat16).astype(F32)
        for i in range(hpg):
            cols = slice(i * qb, (i + 1) * qb)
            si = s[:, cols] + m_add
            s_scr[g][c, :, cols] = si
            mx_scr[g][:, cols] = jnp.maximum(mx_scr[g][:, cols], _fold_rows(si, SUBLANES, jnp.maximum))

    def value_step(g, m, c):
        p = jnp.exp2(s_scr[g][c] - m)
        ls_scr[...] += _fold_rows(p, SUBLANES, jnp.add)
        acc_scr[...] += _dot(vbt_ref[0, c], p.astype(MXU_DTYPE))

    def chunk_loop(step):
        def body(i, carry):
            step(2 * i)
            step(2 * i + 1)
            return carry
        lax.fori_loop(0, nch // 2, body, 0)

        @pl.when(nch % 2 == 1)
        def _():
            step(nch - 1)

    qgs = [jnp.concatenate([qp_ref[0, :, (g * hpg + i) * LANES:(g * hpg + i + 1) * LANES] for i in range(hpg)],
                           axis=0) for g in range(N_KV)]
    for mx in mx_scr:
        mx[...] = jnp.full(mx.shape, NEG_INF, F32)
    chunk_loop(lambda c: logit_step(0, qgs[0], c))
    for g in range(N_KV):
        m = jnp.max(mx_scr[g][...], axis=0, keepdims=True)
        ls_scr[...] = jnp.zeros(ls_scr.shape, F32)
        acc_scr[...] = jnp.zeros(acc_scr.shape, F32)
        if g + 1 < N_KV:
            def both(c, g=g, m=m):
                value_step(g, m, c)
                logit_step(g + 1, qgs[g + 1], c)
            chunk_loop(both)
        else:
            chunk_loop(lambda c, g=g, m=m: value_step(g, m, c))
        o_t = acc_scr[...] / jnp.sum(ls_scr[...], axis=0, keepdims=True)
        feats = slice(g * HEAD_DIM, (g + 1) * HEAD_DIM)
        for pair in range(hpg // 2):
            even = o_t[feats, (2 * pair) * qb:(2 * pair + 1) * qb]
            odd = o_t[feats, (2 * pair + 1) * qb:(2 * pair + 2) * qb]
            tile = g * (hpg // 2) + pair
            o_ref[0, :, tile * LANES:(tile + 1) * LANES] = jnp.concatenate([even, odd], axis=0).T.astype(o_ref.dtype)


def _attn_prompt(qp, qip, wit, ki2, kb, vbt, n_sel):
    b, s, _ = qp.shape
    kc = vbt.shape[3]
    qb = ATT_QUERIES if s % ATT_QUERIES == 0 else Q_BLOCK
    nq = s // qb
    n_bits = max(1, (s - 1).bit_length())
    blk = lambda i, j: (i, j, 0)
    seq = lambda i, j: (i, 0, 0)
    n = (N_HEADS // N_KV) * qb
    return pl.pallas_call(
        functools.partial(_attn_prompt_kernel, n_sel=n_sel, n_bits=n_bits, kc=kc, qb=qb),
        out_shape=jax.ShapeDtypeStruct((b, s, N_HEADS * HEAD_DIM), MXU_DTYPE),
        grid=(b, nq),
        in_specs=[pl.BlockSpec((1, qb, qp.shape[2]), blk),
                  pl.BlockSpec((1, qb, qip.shape[2]), blk),
                  pl.BlockSpec((SUBLANES, qb), lambda i, j: (0, i * nq + j)),
                  pl.BlockSpec((1, s, LANES), seq),
                  pl.BlockSpec((1, s, LANES), seq),
                  pl.BlockSpec((1, s // kc, LANES, kc), lambda i, j: (i, 0, 0, 0))],
        out_specs=pl.BlockSpec((1, qb, N_HEADS * HEAD_DIM), blk),
        scratch_shapes=[pltpu.VMEM((s // kc, kc, qb), I32),
                        pltpu.VMEM((s // kc, kc, qb), I16),
                        pltpu.VMEM((s // kc, kc, qb), I16),
                        pltpu.VMEM((s // kc, kc, n), F32),
                        pltpu.VMEM((s // kc, kc, n), F32),
                        pltpu.VMEM((SUBLANES, n), F32),
                        pltpu.VMEM((SUBLANES, n), F32),
                        pltpu.VMEM((SUBLANES, n), F32),
                        pltpu.VMEM((LANES, n), F32),
                        pltpu.VMEM((SUBLANES, qb), I32)],
        compiler_params=_cparams(2),
        name="attn_prompt",
    )(qp, qip, wit, ki2, kb, vbt)


def _out_kernel(x_ref, yl_ref, ya_ref, mod_ref, g_ref, wo_ref, o_ref, *, lw):
    y = _dot(yl_ref[...], wo_ref[:lw, :]) + _dot(ya_ref[...], wo_ref[lw:, :])
    o_ref[...] = x_ref[...] + mod_ref[0, 5] * _rms(y, g_ref[3:4, :])


def _out_proj(x, y_lru, y_att, mod4, g, wo):
    t, d = x.shape
    lw = y_lru.shape[1]
    tm = _token_tile(t)
    n_tiles = t // tm
    tok = lambda i: (i, 0)
    return pl.pallas_call(
        functools.partial(_out_kernel, lw=lw),
        out_shape=jax.ShapeDtypeStruct((t, d), F32),
        grid=(n_tiles,),
        in_specs=[pl.BlockSpec((tm, d), tok), pl.BlockSpec((tm, lw), tok),
                  pl.BlockSpec((tm, y_att.shape[1]), tok), _mod_spec(mod4, n_tiles),
                  pl.BlockSpec(g.shape, lambda i: (0, 0)), pl.BlockSpec(wo.shape, lambda i: (0, 0))],
        out_specs=pl.BlockSpec((tm, d), tok),
        compiler_params=_cparams(1),
        name="out_proj",
    )(x, y_lru, y_att, mod4, g, wo)


def _attn_sample_kernel(pt_ref, qs_ref, qis_ref, wcol_ref, kin_ref, kn_ref, vn_ref, ckidx_ref, ck_ref, cv_ref,
                        o_ref, kidx_buf, k_buf, v_buf, key_scr, madd_scr, s_scr, jc_scr, sems,
                        *, layer, n_pages, page, ds, n_sel, n_bits, kc):
    bi = pl.program_id(0)
    past = n_pages * page
    total = past + LANES
    n_chunks = past // kc
    slot = bi % 2

    streams = ((ckidx_ref, kidx_buf), (ck_ref, k_buf), (cv_ref, v_buf))

    def page_copy(which, seq, slot_, p):
        src, dst = streams[which]
        cols = pl.ds(pl.multiple_of(p * page, page), page)
        return pltpu.make_async_copy(src.at[layer, pt_ref[seq, p]], dst.at[slot_, :, cols], sems.at[slot_, which])

    def fetch(seq, slot_):
        def body(p, carry):
            for which in range(len(streams)):
                page_copy(which, seq, slot_, p).start()
            return carry
        lax.fori_loop(0, n_pages, body, 0)

    def wait_stream(which):
        dst = streams[which][1].at[slot, :, 0:past]
        pltpu.make_async_copy(dst, dst, sems.at[slot, which]).wait()

    @pl.when(bi == 0)
    def _():
        fetch(bi, slot)

    @pl.when(bi + 1 < pl.num_programs(0))
    def _():
        fetch(bi + 1, 1 - slot)

    kidx_buf[slot, :, past:total] = kin_ref[0]
    k_buf[slot, :, past:total] = kn_ref[0]
    v_buf[slot, :, past:total] = vn_ref[0]
    key_scr[...] = jnp.full(key_scr.shape, INT_MIN, I32)

    qis = qis_ref[0]
    wcol = wcol_ref[0][:, 0:1]
    wait_stream(0)

    def scores(lo_col, width):
        d = _dot(qis, kidx_buf[slot, :, lo_col:lo_col + width].astype(MXU_DTYPE))
        wgt = jnp.maximum(d, 0.0) * wcol
        sc = wgt[0:ds]
        for h in range(1, IDX_HEADS):
            sc = sc + wgt[h * ds:(h + 1) * ds]
        return sc

    top = SUBLANES // 2
    half = past // 2
    hw = key_scr.shape[1]

    def place(lo_col):
        return (0, lo_col) if lo_col < half else (top, lo_col - half)

    for c in range(n_chunks):
        r0, c0 = place(c * kc)
        key_scr[r0:r0 + ds, c0:c0 + kc] = _score_key(scores(c * kc, kc))
    sc_new = scores(past, LANES)
    lane_new = lax.broadcasted_iota(I32, (ds, LANES), 1)
    row_new = lax.broadcasted_iota(I32, (ds, LANES), 0)
    r0, c0 = place(past)
    key_scr[r0:r0 + ds, c0:c0 + LANES] = jnp.where(lane_new > row_new, jnp.int32(INT_MIN), _score_key(sc_new))

    rows = key_scr.shape[0]
    lane = lax.broadcasted_iota(I32, (rows, LANES), 1)
    col0 = lane + jnp.where(lax.broadcasted_iota(I32, (rows, LANES), 0) >= top, half, 0)

    def count(ind, *row_args):
        args = [jnp.broadcast_to(a, (rows, LANES)) for a in row_args]
        accs = None
        for i in range(hw // LANES):
            vals = ind(key_scr[:, i * LANES:(i + 1) * LANES], col0 + i * LANES, *args)
            accs = vals if accs is None else tuple(a + v for a, v in zip(accs, vals))
        return tuple(jnp.sum(a + pltpu.roll(a, top, 0), axis=1, keepdims=True) for a in accs)

    def count_ge3(c1, c2, c3):
        return count(lambda k, col, b1, b2, b3: (jnp.where(k >= b1, 1.0, 0.0), jnp.where(k >= b2, 1.0, 0.0),
                                                 jnp.where(k >= b3, 1.0, 0.0)), c1, c2, c3)

    t = _kth_largest_key_radix4(count_ge3, (rows, 1), n_sel)
    cnt_gt, cnt_ge = count(lambda k, col, tb: (jnp.where(k > tb, 1.0, 0.0), jnp.where(k >= tb, 1.0, 0.0)), t)
    need = n_sel - cnt_gt
    live = t != jnp.int32(INT_MIN)
    excess = jnp.where(jnp.logical_and(live, cnt_ge > n_sel), 1.0, 0.0)
    jc_scr[...] = jnp.broadcast_to(jnp.where(live, jnp.int32(2 ** 30), jnp.int32(-1)), jc_scr.shape)

    @pl.when(jnp.max(excess) > 0.0)
    def _():
        def ties_below(cand):
            return count(lambda k, col, tb, cb: (jnp.where(k == tb, jnp.where(col < cb, 1.0, 0.0), 0.0),), t, cand)[0]

        jcut = _tie_cutoff(ties_below, need, (rows, 1), n_bits)
        jc_scr[...] = jnp.broadcast_to(jnp.where(live, jcut, jnp.int32(-1)), jc_scr.shape)

    tb, jb = jnp.broadcast_to(t, (rows, LANES)), jc_scr[...]
    for i in range(hw // LANES):
        cols = slice(i * LANES, (i + 1) * LANES)
        madd_scr[:, cols] = _mask_add(key_scr[:, cols], col0 + i * LANES, tb, jb)

    qs = qs_ref[0]
    nrow = qs.shape[0]

    def row_mask(lo_col, width):
        r0, c0 = place(lo_col)
        return jnp.concatenate(
            [jnp.broadcast_to(madd_scr[r0 + tt:r0 + tt + 1, c0:c0 + width], (N_HEADS, width)) for tt in range(ds)],
            axis=0)

    spans = [(c * kc, kc) for c in range(n_chunks)] + [(past, LANES)]
    wait_stream(1)
    m = jnp.full((nrow, 1), NEG_INF, F32)
    for lo_col, width in spans:
        s = _dot(qs, k_buf[slot, :, lo_col:lo_col + width].astype(MXU_DTYPE)) + row_mask(lo_col, width)
        s_scr[:, lo_col:lo_col + width] = s
        m = jnp.maximum(m, jnp.max(s, axis=1, keepdims=True))
    wait_stream(2)
    l = jnp.zeros((nrow, 1), F32)
    acc = jnp.zeros((nrow, LANES), F32)
    for lo_col, width in spans:
        p = jnp.exp2(s_scr[:, lo_col:lo_col + width] - m)
        l = l + jnp.sum(p, axis=1, keepdims=True)
        acc = acc + _dot_nt(p.astype(MXU_DTYPE), v_buf[slot, :, lo_col:lo_col + width].astype(MXU_DTYPE))
    o_ref[0] = acc / l


def _attn_sample(page_table, qs, qis, wcol, ki_new, k_new, v_new, cache_kidx, cache_k2, cache_v2, layer, n_sel, ds):
    db, n_pages = page_table.shape
    page = cache_kidx.shape[3]
    past = n_pages * page
    total = past + LANES
    kc = 2048 if past % 4096 == 0 else page
    assert (past // kc) % 2 == 0 and 2 * ds <= SUBLANES
    n_bits = max(1, (total - 1).bit_length())
    per = lambda i, pt: (i, 0, 0)
    any_spec = pl.BlockSpec(memory_space=pl.ANY)
    grid_spec = pltpu.PrefetchScalarGridSpec(
        num_scalar_prefetch=1,
        grid=(db,),
        in_specs=[pl.BlockSpec((1,) + qs.shape[1:], per), pl.BlockSpec((1,) + qis.shape[1:], per),
                  pl.BlockSpec((1,) + wcol.shape[1:], per), pl.BlockSpec((1,) + ki_new.shape[1:], per),
                  pl.BlockSpec((1,) + k_new.shape[1:], per), pl.BlockSpec((1,) + v_new.shape[1:], per),
                  any_spec, any_spec, any_spec],
        out_specs=pl.BlockSpec((1, qs.shape[1], LANES), per),
        scratch_shapes=[pltpu.VMEM((2, IDX_DIM, total), F32),
                        pltpu.VMEM((2, LANES, total), F32),
                        pltpu.VMEM((2, LANES, total), F32),
                        pltpu.VMEM((SUBLANES, past // 2 + LANES), I32),
                        pltpu.VMEM((SUBLANES, past // 2 + LANES), F32),
                        pltpu.VMEM((qs.shape[1], total), F32),
                        pltpu.VMEM((SUBLANES, LANES), I32),
                        pltpu.SemaphoreType.DMA((2, 3))])
    return pl.pallas_call(
        functools.partial(_attn_sample_kernel, layer=layer, n_pages=n_pages, page=page, ds=ds, n_sel=n_sel,
                          n_bits=n_bits, kc=kc),
        out_shape=jax.ShapeDtypeStruct((db, qs.shape[1], LANES), F32),
        grid_spec=grid_spec,
        compiler_params=_cparams(1),
        name="attn_sample",
    )(page_table, qs, qis, wcol, ki_new, k_new, v_new, cache_kidx, cache_k2, cache_v2)


def _rope_tables(pos):
    half = ROPE_DIM // 2
    freqs = ROPE_THETA ** (-(jnp.arange(half, dtype=F32) / half))
    ang = pos.astype(F32)[:, None] * freqs[None, :]
    cos, sin = jnp.cos(ang), jnp.sin(ang)
    rest = HEAD_DIM - ROPE_DIM
    c64 = jnp.concatenate([cos, cos, jnp.ones((pos.shape[0], rest), F32)], axis=1)
    s64 = jnp.concatenate([-sin, sin, jnp.zeros((pos.shape[0], rest), F32)], axis=1)
    return jnp.tile(c64, (1, LANES // HEAD_DIM)), jnp.tile(s64, (1, LANES // HEAD_DIM))


def _block_diag(w):
    nb, n, _ = w.shape
    eye = jnp.eye(nb, dtype=w.dtype)
    return (eye[:, None, :, None] * w[:, :, None, :]).reshape(nb * n, nb * n)


def kernel(x_prompt, x_sample, c_prompt, c_sample, cache_k, cache_v, cache_kidx, state_h, state_conv, page_table,
           w_ada, b_ada, norm_g, ffn_w_gu, ffn_w_down, w_in, conv_w, conv_b, lru_wa, lru_ba, lru_wx, lru_bx,
           lru_lambda, w_out):
    b, s, d = x_prompt.shape
    db, ds, _ = x_sample.shape
    depth = w_ada.shape[0]
    lw = lru_lambda.shape[1]
    n_pool, page = cache_k.shape[1], cache_k.shape[2]
    past = page_table.shape[1] * page
    kv_w = N_KV * HEAD_DIM
    tm_p = _token_tile(b * s)
    assert kv_w == LANES and 2 * IDX_DIM == LANES and s % tm_p == 0 and tm_p % Q_BLOCK == 0 and page % LANES == 0

    cos_p, sin_p = _rope_tables(jnp.arange(s, dtype=I32))
    cos_s, sin_s = _rope_tables(past + jnp.arange(ds, dtype=I32))
    cos_s, sin_s = jnp.tile(cos_s, (db, 1)), jnp.tile(sin_s, (db, 1))

    n_c = b + db
    c_all = jnp.concatenate([c_prompt, c_sample, jnp.zeros((-n_c % SUBLANES, d), F32)], axis=0)
    cache_kidx_t = cache_kidx.transpose(0, 1, 3, 2)
    cache_k_t = cache_k.transpose(0, 1, 3, 4, 2).reshape(depth, n_pool, kv_w, page)
    cache_v_t = cache_v.transpose(0, 1, 3, 4, 2).reshape(depth, n_pool, kv_w, page)

    def new_cols(a):
        a = a.reshape(db, ds, a.shape[-1]).transpose(0, 2, 1)
        return jnp.pad(a, ((0, 0), (0, 0), (0, LANES - ds)))

    xp = x_prompt.reshape(b * s, d)
    xs = x_sample.reshape(db * ds, d)
    n_sel_p = min(TOP_K, s // 4)
    n_sel_s = min(TOP_K, (past + ds) // 4)
    q_w = N_HEADS * HEAD_DIM
    qi_w = IDX_HEADS * IDX_DIM
    col_ki = 2 * lw + q_w + 2 * kv_w + qi_w

    outs = {name: [] for name in ("kp", "vp", "kip", "hp", "cp", "ks", "vs", "kis", "hs", "cs")}
    for l in range(depth):
        mod = _ada_mod(c_all, w_ada[l], b_ada[l])
        mod_p = mod[:b].reshape(b, N_MOD, 1, d)
        mod_s = jnp.repeat(mod[b:n_c].reshape(db, N_MOD, d), ds, axis=0).transpose(1, 0, 2)[None]
        g = norm_g[l]
        wgu = ffn_w_gu[l].astype(MXU_DTYPE)
        wdn = ffn_w_down[l].astype(MXU_DTYPE)
        w_ki = w_in[l][:, col_ki:col_ki + IDX_DIM]
        w_pad = jnp.concatenate(
            [w_in[l][:, :col_ki], w_ki, w_ki, w_in[l][:, col_ki + IDX_DIM:],
             jnp.zeros((d, LANES - IDX_HEADS), F32)], axis=1).astype(MXU_DTYPE)
        wo = w_out[l].astype(MXU_DTYPE)
        wa_bd = _block_diag(lru_wa[l]).astype(MXU_DTYPE)
        wx_bd = _block_diag(lru_wx[l]).astype(MXU_DTYPE)
        lru_args = (conv_w[l], conv_b[l].reshape(1, lw), wa_bd, lru_ba[l].reshape(1, lw), wx_bd,
                    lru_bx[l].reshape(1, lw), lru_lambda[l].reshape(1, lw))

        xp = _ffn(xp, mod_p, g, wgu[0], wdn[0], 0, 0)
        xl, gl, qp, k, v, kb, vbt, qip, ki, ki2, _, wit = _proj(xp, mod_p, g, w_pad, cos_p, sin_p, lw, s // tm_p)
        r3 = lambda a: a.reshape(b, s, a.shape[-1])
        y_lru, h_t, nbuf = _lru_prompt(r3(xl), r3(gl), jnp.zeros((b, CONV_W - 1, lw), F32), jnp.zeros((b, lw), F32),
                                       *lru_args)
        y_att = _attn_prompt(r3(qp), r3(qip), wit, r3(ki2), r3(kb), vbt.reshape(b, s // tm_p, LANES, tm_p), n_sel_p)
        xp = _out_proj(xp, y_lru.reshape(b * s, lw), y_att.reshape(b * s, q_w), mod_p, g, wo)
        xp = _ffn(xp, mod_p, g, wgu[1], wdn[1], 6, 4)
        outs["kp"].append(k.reshape(b, s, N_KV, HEAD_DIM))
        outs["vp"].append(v.reshape(b, s, N_KV, HEAD_DIM))
        outs["kip"].append(ki.reshape(b, s, IDX_DIM))
        outs["hp"].append(h_t.reshape(b, lw))
        outs["cp"].append(nbuf)

        xs = _ffn(xs, mod_s, g, wgu[0], wdn[0], 0, 0)
        xl, gl, qp, k, v, _, _, qip, ki, _, wi, _ = _proj(xs, mod_s, g, w_pad, cos_s, sin_s, lw, 1)
        tmaj = lambda a: a.reshape(db, ds, a.shape[-1]).transpose(1, 0, 2)
        y_lru, h_t, nbuf = _lru_sample(tmaj(xl), tmaj(gl), state_conv[l].transpose(1, 0, 2), state_h[l], *lru_args)
        qs = qp.reshape(db, ds * N_HEADS, LANES)
        qis = qip.reshape(db, ds, IDX_HEADS, 2, IDX_DIM)
        qis = jnp.stack([qis[:, :, h, h % 2] for h in range(IDX_HEADS)], axis=1).reshape(db, IDX_HEADS * ds, IDX_DIM)
        wcol = wi.reshape(db, ds, LANES)[:, :, :IDX_HEADS].transpose(0, 2, 1).reshape(db, IDX_HEADS * ds, 1)
        wcol = jnp.broadcast_to(wcol, (db, IDX_HEADS * ds, LANES))
        o_s = _attn_sample(page_table, qs, qis, wcol, new_cols(ki), new_cols(k), new_cols(v),
                           cache_kidx_t, cache_k_t, cache_v_t, l, n_sel_s, ds)
        o_s = o_s.reshape(db, ds, N_HEADS, N_KV, HEAD_DIM)
        hpg = N_HEADS // N_KV
        y_att = jnp.stack([o_s[:, :, h, h // hpg] for h in range(N_HEADS)], axis=2).reshape(db * ds, q_w)
        xs = _out_proj(xs, y_lru.transpose(1, 0, 2).reshape(db * ds, lw), y_att.astype(MXU_DTYPE), mod_s, g, wo)
        xs = _ffn(xs, mod_s, g, wgu[1], wdn[1], 6, 4)
        outs["ks"].append(k.reshape(db, ds, N_KV, HEAD_DIM))
        outs["vs"].append(v.reshape(db, ds, N_KV, HEAD_DIM))
        outs["kis"].append(ki.reshape(db, ds, IDX_DIM))
        outs["hs"].append(h_t)
        outs["cs"].append(nbuf.transpose(1, 0, 2))

    st = lambda name: jnp.stack(outs[name])
    return (xp.reshape(b, s, d), xs.reshape(db, ds, d), st("kp"), st("vp"), st("kip"), st("hp"), st("cp"),
            st("ks"), st("vs"), st("kis"), st("hs"), st("cs"))
```

```python
import functools
import math

import jax
import jax.numpy as jnp
from jax import lax
from jax.experimental import pallas as pl
from jax.experimental.pallas import tpu as pltpu

F32 = jnp.float32
I32 = jnp.int32
I16 = jnp.int16
MXU_DTYPE = jnp.bfloat16

LRU_BLOCKS = 8
CONV_W = 4
LRU_C = 8.0
N_HEADS = 8
HEAD_DIM = 64
N_KV = 2
ROPE_DIM = HEAD_DIM // 4
ROPE_THETA = 500000.0
IDX_HEADS = 4
IDX_DIM = 64
TOP_K = 256
Q_BLOCK = 128
N_MOD = 9
RMS_EPS = 1e-6
LOG2E = math.log2(math.e)

LANES = 128
SUBLANES = 8
PACKED_SUBLANES = 16
TOKEN_TILE = 512
ATT_QUERIES = 256
VMEM_LIMIT = 56 * 1024 * 1024
INT_MIN = -2147483648
HALF_RANGE = 32768
NEG_INF = float("-inf")


def _cparams(n_axes):
    return pltpu.CompilerParams(dimension_semantics=("arbitrary",) * n_axes,
                                vmem_limit_bytes=VMEM_LIMIT)


def _rms(x, g):
    ms = jnp.mean(x * x, axis=-1, keepdims=True)
    return x * lax.rsqrt(ms + RMS_EPS) * g


def _dot(a, b):
    return jnp.dot(a, b, preferred_element_type=F32)


def _dot_nt(a, b):
    return lax.dot_general(a, b, (((1,), (1,)), ((), ())), preferred_element_type=F32)


def _mod_kernel(c_ref, w_ref, b_ref, o_ref):
    o_ref[...] = jnp.dot(c_ref[...], w_ref[...], preferred_element_type=F32,
                         precision=lax.Precision.HIGHEST) + b_ref[...]


def _ada_mod(c_all, w, b):
    rows, d = c_all.shape
    n = w.shape[1]
    tn = d
    return pl.pallas_call(
        _mod_kernel,
        out_shape=jax.ShapeDtypeStruct((rows, n), F32),
        grid=(n // tn,),
        in_specs=[pl.BlockSpec((rows, d), lambda i: (0, 0)),
                  pl.BlockSpec((d, tn), lambda i: (0, i)),
                  pl.BlockSpec((1, tn), lambda i: (0, i))],
        out_specs=pl.BlockSpec((rows, tn), lambda i: (0, i)),
        compiler_params=_cparams(1),
        name="ada_mod",
    )(c_all, w, b.reshape(1, n))


def _ffn_kernel(x_ref, mod_ref, g_ref, wgu_ref, wdn_ref, o_ref, *, m0, gi, d_ff):
    x = x_ref[...]
    sh, sc, gt = mod_ref[0, m0], mod_ref[0, m0 + 1], mod_ref[0, m0 + 2]
    h = (_rms(x, g_ref[gi:gi + 1, :]) * (1.0 + sc) + sh).astype(MXU_DTYPE)
    g = _dot(h, wgu_ref[:, :d_ff])
    u = _dot(h, wgu_ref[:, d_ff:])
    a = (g * jax.nn.sigmoid(g) * u).astype(MXU_DTYPE)
    f = _dot(a, wdn_ref[...])
    o_ref[...] = x + 0.5 * gt * _rms(f, g_ref[gi + 1:gi + 2, :])


def _token_tile(t):
    return TOKEN_TILE if t % TOKEN_TILE == 0 else t


def _mod_spec(mod4, n_tiles):
    nb, _, r, d = mod4.shape
    per = n_tiles // nb
    return pl.BlockSpec((1, N_MOD, r, d), lambda i: (i // per, 0, 0, 0))


def _ffn(x, mod4, g, wgu, wdn, m0, gi):
    t, d = x.shape
    d_ff = wdn.shape[0]
    tm = _token_tile(t)
    n_tiles = t // tm
    return pl.pallas_call(
        functools.partial(_ffn_kernel, m0=m0, gi=gi, d_ff=d_ff),
        out_shape=jax.ShapeDtypeStruct((t, d), F32),
        grid=(n_tiles,),
        in_specs=[pl.BlockSpec((tm, d), lambda i: (i, 0)),
                  _mod_spec(mod4, n_tiles),
                  pl.BlockSpec(g.shape, lambda i: (0, 0)),
                  pl.BlockSpec(wgu.shape, lambda i: (0, 0)),
                  pl.BlockSpec(wdn.shape, lambda i: (0, 0))],
        out_specs=pl.BlockSpec((tm, d), lambda i: (i, 0)),
        compiler_params=_cparams(1),
        name="ffn",
    )(x, mod4, g, wgu, wdn)


def _rope(x, cos, sin):
    lane = lax.broadcasted_iota(I32, x.shape, 1)
    first = (lane & (HEAD_DIM - 1)) < (ROPE_DIM // 2)
    partner = jnp.where(first, pltpu.roll(x, LANES - ROPE_DIM // 2, 1), pltpu.roll(x, ROPE_DIM // 2, 1))
    return x * cos + partner * sin


def _proj_kernel(x_ref, mod_ref, g_ref, w_ref, cos_ref, sin_ref,
                 xl_ref, gl_ref, qp_ref, k_ref, v_ref, kb_ref, vbt_ref, qip_ref, ki_ref, ki2_ref, wi_ref, wit_ref,
                 *, lw):
    x = x_ref[...]
    h = (_rms(x, g_ref[2:3, :]) * (1.0 + mod_ref[0, 4]) + mod_ref[0, 3]).astype(MXU_DTYPE)
    p = _dot(h, w_ref[...])
    cos, sin = cos_ref[...], sin_ref[...]
    lane = lax.broadcasted_iota(I32, cos.shape, 1)
    lo = lane < HEAD_DIM

    o = 0
    xl_ref[...] = p[:, o:o + lw]
    o += lw
    gl_ref[...] = p[:, o:o + lw]
    o += lw
    q_scale = HEAD_DIM ** -0.5 * LOG2E
    for pair in range(N_HEADS // 2):
        qr = _rope(p[:, o + pair * LANES:o + (pair + 1) * LANES], cos, sin) * q_scale
        qr_sw = pltpu.roll(qr, HEAD_DIM, 1)
        for half in range(2):
            hd = 2 * pair + half
            grp = hd // (N_HEADS // N_KV)
            src = qr if half == grp else qr_sw
            slab = jnp.where(lo if grp == 0 else jnp.logical_not(lo), src, 0.0)
            qp_ref[:, hd * LANES:(hd + 1) * LANES] = slab.astype(qp_ref.dtype)
    o += N_HEADS * HEAD_DIM
    k = _rope(p[:, o:o + LANES], cos, sin)
    k_ref[...] = k
    kb_ref[...] = k.astype(kb_ref.dtype)
    o += LANES
    v = p[:, o:o + LANES]
    v_ref[...] = v
    vbt_ref[0] = v.T.astype(vbt_ref.dtype)
    o += LANES
    for pair in range(IDX_HEADS // 2):
        qi = _rope(p[:, o + pair * LANES:o + (pair + 1) * LANES], cos, sin)
        for half in range(2):
            hd = 2 * pair + half
            slab = jnp.where(lo if half == 0 else jnp.logical_not(lo), qi, 0.0)
            qip_ref[:, hd * LANES:(hd + 1) * LANES] = slab.astype(qip_ref.dtype)
    o += IDX_HEADS * IDX_DIM
    ki2 = _rope(p[:, o:o + LANES], cos, sin)
    ki_ref[...] = ki2[:, :IDX_DIM]
    ki2_ref[...] = ki2.astype(ki2_ref.dtype)
    o += LANES
    wi = p[:, o:o + LANES] * (IDX_DIM ** -0.5 * IDX_HEADS ** -0.5)
    wi_ref[...] = wi
    wit_ref[...] = wi.T[:SUBLANES, :]


def _proj(x, mod4, g, w_pad, cos, sin, lw, pos_tiles):
    t, d = x.shape
    tm = _token_tile(t)
    n_tiles = t // tm
    tok = lambda i: (i, 0)
    pos = lambda i: (i % pos_tiles, 0)
    row_major = [(lw, F32), (lw, F32), (N_HEADS * LANES, MXU_DTYPE), (LANES, F32), (LANES, F32),
                 (LANES, MXU_DTYPE), None, (IDX_HEADS * LANES, MXU_DTYPE),
                 (IDX_DIM, F32), (LANES, MXU_DTYPE), (LANES, F32), None]
    out_shape, out_specs = [], []
    for idx, entry in enumerate(row_major):
        if entry is not None:
            out_shape.append(jax.ShapeDtypeStruct((t, entry[0]), entry[1]))
            out_specs.append(pl.BlockSpec((tm, entry[0]), tok))
        elif idx == 6:
            out_shape.append(jax.ShapeDtypeStruct((n_tiles, LANES, tm), MXU_DTYPE))
            out_specs.append(pl.BlockSpec((1, LANES, tm), lambda i: (i, 0, 0)))
        else:
            out_shape.append(jax.ShapeDtypeStruct((SUBLANES, t), F32))
            out_specs.append(pl.BlockSpec((SUBLANES, tm), lambda i: (0, i)))
    return pl.pallas_call(
        functools.partial(_proj_kernel, lw=lw),
        out_shape=out_shape,
        grid=(n_tiles,),
        in_specs=[pl.BlockSpec((tm, d), tok),
                  _mod_spec(mod4, n_tiles),
                  pl.BlockSpec(g.shape, lambda i: (0, 0)),
                  pl.BlockSpec(w_pad.shape, lambda i: (0, 0)),
                  pl.BlockSpec((tm, LANES), pos),
                  pl.BlockSpec((tm, LANES), pos)],
        out_specs=out_specs,
        compiler_params=_cparams(1),
        name="mixer_proj",
    )(x, mod4, g, w_pad, cos, sin)


def _gelu_tanh(x):
    return x * (0.5 * (1.0 + jnp.tanh(0.7978845608028654 * (x + 0.044715 * (x * x * x)))))


def _lru_coeffs(xc, wa_ref, ba_ref, wx_ref, bx_ref, lam_ref):
    xcb = xc.astype(MXU_DTYPE)
    r = jax.nn.sigmoid(_dot(xcb, wa_ref[...]) + ba_ref[...])
    i = jax.nn.sigmoid(_dot(xcb, wx_ref[...]) + bx_ref[...])
    nl = -lam_ref[...]
    softplus = jnp.maximum(nl, 0.0) + jnp.log1p(jnp.exp(-jnp.abs(nl)))
    log_a = (-LRU_C) * r * softplus
    a = jnp.exp(log_a)
    one_minus_a2 = -jnp.tanh(log_a) * (a * a + 1.0)
    return a, jnp.sqrt(one_minus_a2) * (i * xc)


def _lru_prompt_kernel(xl_ref, gl_ref, buf_ref, h0_ref, cw_ref, cb_ref, wa_ref, ba_ref, wx_ref, bx_ref, lam_ref,
                       y_ref, ht_ref, nb_ref, xs_scr, a_scr, b_scr, h_scr, *, ts):
    tj = pl.program_id(1)
    pad = SUBLANES
    ng = ts // SUBLANES

    @pl.when(tj == 0)
    def _():
        xs_scr[0:pad, :] = jnp.zeros((pad, xs_scr.shape[1]), F32)
        xs_scr[pad - (CONV_W - 1):pad, :] = buf_ref[0]
        h_scr[...] = h0_ref[0]

    x = xl_ref[0]
    xs_scr[pad:pad + ts, :] = x
    xc = cb_ref[...] + x * cw_ref[CONV_W - 1:CONV_W, :]
    for j in range(1, CONV_W):
        xc = xc + xs_scr[pad - j:pad - j + ts, :] * cw_ref[CONV_W - 1 - j:CONV_W - j, :]

    a, b = _lru_coeffs(xc, wa_ref, ba_ref, wx_ref, bx_ref, lam_ref)
    lw = a.shape[1]
    a = a.reshape(ng, SUBLANES, lw)
    b = b.reshape(ng, SUBLANES, lw)
    sub = lax.broadcasted_iota(I32, (ng, SUBLANES, lw), 1)
    d = 1
    while d < SUBLANES:
        keep = sub >= d
        a_sh = jnp.where(keep, pltpu.roll(a, d, 1), 1.0)
        b_sh = jnp.where(keep, pltpu.roll(b, d, 1), 0.0)
        b = a * b_sh + b
        a = a * a_sh
        d *= 2
    a_scr[...] = a
    b_scr[...] = b

    def carry(gidx, h):
        hg = a_scr[gidx] * h + b_scr[gidx]
        b_scr[gidx] = hg
        return hg[SUBLANES - 1:SUBLANES, :]

    h_last = lax.fori_loop(0, ng, carry, h_scr[...], unroll=8)
    h_scr[...] = h_last
    hs = b_scr[...].reshape(ts, lw)
    y_ref[0] = (hs * _gelu_tanh(gl_ref[0])).astype(y_ref.dtype)
    xs_scr[0:pad, :] = xs_scr[ts:ts + pad, :]

    @pl.when(tj == pl.num_programs(1) - 1)
    def _():
        ht_ref[0] = h_last
        nb_ref[0] = xs_scr[pad - (CONV_W - 1):pad, :]


def _lru_prompt(xl, gl, buf0, h0, cw, cb, wa_bd, ba, wx_bd, bx, lam):
    b, s, lw = xl.shape
    ts = _token_tile(s)
    seq = lambda i, j: (i, j, 0)
    per_b = lambda i, j: (i, 0, 0)
    const = lambda i, j: (0, 0)
    row = pl.BlockSpec((1, lw), const)
    return pl.pallas_call(
        functools.partial(_lru_prompt_kernel, ts=ts),
        out_shape=[jax.ShapeDtypeStruct((b, s, lw), MXU_DTYPE),
                   jax.ShapeDtypeStruct((b, 1, lw), F32),
                   jax.ShapeDtypeStruct((b, CONV_W - 1, lw), F32)],
        grid=(b, s // ts),
        in_specs=[pl.BlockSpec((1, ts, lw), seq), pl.BlockSpec((1, ts, lw), seq),
                  pl.BlockSpec((1, CONV_W - 1, lw), per_b), pl.BlockSpec((1, 1, lw), per_b),
                  pl.BlockSpec((CONV_W, lw), const), row,
                  pl.BlockSpec((lw, lw), const), row, pl.BlockSpec((lw, lw), const), row, row],
        out_specs=[pl.BlockSpec((1, ts, lw), seq), pl.BlockSpec((1, 1, lw), per_b),
                   pl.BlockSpec((1, CONV_W - 1, lw), per_b)],
        scratch_shapes=[pltpu.VMEM((ts + SUBLANES, lw), F32),
                        pltpu.VMEM((ts // SUBLANES, SUBLANES, lw), F32),
                        pltpu.VMEM((ts // SUBLANES, SUBLANES, lw), F32),
                        pltpu.VMEM((1, lw), F32)],
        compiler_params=_cparams(2),
        name="rglru_prompt",
    )(xl, gl, buf0, h0.reshape(b, 1, lw), cw, cb, wa_bd, ba, wx_bd, bx, lam)


def _lru_sample_kernel(xl_ref, gl_ref, buf_ref, h0_ref, cw_ref, cb_ref, wa_ref, ba_ref, wx_ref, bx_ref, lam_ref,
                       y_ref, ht_ref, nb_ref, *, ds):
    rows = [buf_ref[j] for j in range(CONV_W - 1)] + [xl_ref[t] for t in range(ds)]
    xcs = []
    for t in range(ds):
        xc = cb_ref[...]
        for j in range(CONV_W):
            xc = xc + rows[t + j] * cw_ref[j:j + 1, :]
        xcs.append(xc)
    db = xcs[0].shape[0]
    a, b = _lru_coeffs(jnp.concatenate(xcs, axis=0), wa_ref, ba_ref, wx_ref, bx_ref, lam_ref)
    h = h0_ref[...]
    for t in range(ds):
        h = a[t * db:(t + 1) * db] * h + b[t * db:(t + 1) * db]
        y_ref[t] = (h * _gelu_tanh(gl_ref[t])).astype(y_ref.dtype)
    ht_ref[...] = h
    for j in range(CONV_W - 1):
        nb_ref[j] = rows[ds + j]


def _lru_sample(xl_t, gl_t, buf_t, h0, cw, cb, wa_bd, ba, wx_bd, bx, lam):
    ds, db, lw = xl_t.shape
    full = lambda a: pl.BlockSpec(a.shape, lambda i, n=a.ndim: (0,) * n)
    args = (xl_t, gl_t, buf_t, h0, cw, cb, wa_bd, ba, wx_bd, bx, lam)
    outs = [jax.ShapeDtypeStruct((ds, db, lw), MXU_DTYPE), jax.ShapeDtypeStruct((db, lw), F32),
            jax.ShapeDtypeStruct((CONV_W - 1, db, lw), F32)]
    return pl.pallas_call(
        functools.partial(_lru_sample_kernel, ds=ds),
        out_shape=outs,
        grid=(1,),
        in_specs=[full(a) for a in args],
        out_specs=[full(o) for o in outs],
        compiler_params=_cparams(1),
        name="rglru_sample",
    )(*args)


def _score_key(sc):
    bits = lax.bitcast_convert_type(sc, I32)
    key = jnp.where(bits < 0, bits ^ jnp.int32(0x7FFFFFFF), bits)
    return jnp.where(sc == 0.0, 0, key)


def _kth_largest_key(count_ge, shape, n_sel):
    def bit_body(i, t_u):
        cand_u = t_u | jnp.left_shift(jnp.int32(1), 31 - i)
        cnt = count_ge(cand_u ^ jnp.int32(INT_MIN))
        return jnp.where(cnt >= n_sel, cand_u, t_u)

    t_u = lax.fori_loop(0, 32, bit_body, jnp.zeros(shape, I32))
    return t_u ^ jnp.int32(INT_MIN)


def _kth_largest_key_radix4(count_ge3, shape, n_sel):
    int_min = jnp.int32(INT_MIN)

    def step(i, t_u):
        sh = 30 - 2 * i
        c1, c2, c3 = (t_u | jnp.left_shift(jnp.int32(d), sh) for d in (1, 2, 3))
        n1, n2, n3 = count_ge3(c1 ^ int_min, c2 ^ int_min, c3 ^ int_min)
        return jnp.where(n3 >= n_sel, c3, jnp.where(n2 >= n_sel, c2, jnp.where(n1 >= n_sel, c1, t_u)))

    t_u = lax.fori_loop(0, 16, step, jnp.zeros(shape, I32))
    return t_u ^ int_min


def _tie_cutoff(count_tie_below, need, shape, n_bits):
    def bit_body(i, j_u):
        cand = j_u | jnp.left_shift(jnp.int32(1), n_bits - 1 - i)
        return jnp.where(count_tie_below(cand) < need, cand, j_u)

    return lax.fori_loop(0, n_bits, bit_body, jnp.zeros(shape, I32))


def _mask_add(key, idx, t, jcut):
    tie = jnp.where(idx <= jcut, 0.0, NEG_INF)
    return jnp.where(key > t, 0.0, jnp.where(key == t, tie, NEG_INF))


def _fold_rows(x, group, op):
    slabs = [x[i * group:(i + 1) * group] for i in range(x.shape[0] // group)]
    chains = slabs[:4]
    for i, slab in enumerate(slabs[4:]):
        chains[i % len(chains)] = op(chains[i % len(chains)], slab)
    while len(chains) > 1:
        chains = [op(chains[i], chains[i + 1]) if i + 1 < len(chains) else chains[i]
                  for i in range(0, len(chains), 2)]
    return chains[0]


def _attn_prompt_kernel(qp_ref, qip_ref, wit_ref, ki2_ref, kb_ref, vbt_ref, o_ref,
                        key_scr, hi_scr, lo_scr, s0_scr, s1_scr, mx0_scr, mx1_scr, ls_scr, acc_scr, jc_scr,
                        *, n_sel, n_bits, kc, qb):
    s_scr, mx_scr = (s0_scr, s1_scr), (mx0_scr, mx1_scr)
    jb = pl.program_id(1)
    nch = ((jb + 1) * qb - 1) // kc + 1
    hpg = N_HEADS // N_KV
    n = hpg * qb

    def key_chunk(ref, c):
        return ref[0, pl.ds(pl.multiple_of(c * kc, kc), kc), :]

    def key_index(c):
        return c * kc + lax.broadcasted_iota(I32, (kc, qb), 0)

    qpos = jb * qb + lax.broadcasted_iota(I32, (kc, qb), 1)

    qi_all = qip_ref[0]
    qi_stack = jnp.concatenate([qi_all[:, h * LANES:(h + 1) * LANES] for h in range(IDX_HEADS)], axis=0)
    wt = wit_ref[...]

    def score_chunk(c, causal_edge):
        d = _dot_nt(key_chunk(ki2_ref, c), qi_stack)
        sc = jnp.maximum(d[:, 0:qb], 0.0) * wt[0:1, :]
        for h in range(1, IDX_HEADS):
            sc = sc + jnp.maximum(d[:, h * qb:(h + 1) * qb], 0.0) * wt[h:h + 1, :]
        key = _score_key(sc)
        if causal_edge:
            key = jnp.where(key_index(c) > qpos, jnp.int32(INT_MIN), key)
        key_scr[c] = key
        hi_scr[c] = lax.shift_right_arithmetic(key, 16).astype(I16)
        lo_scr[c] = ((key & 0xFFFF) - HALF_RANGE).astype(I16)

    n_full = (jb * qb) // kc

    def full_body(c, carry):
        score_chunk(c, False)
        return carry

    def edge_body(c, carry):
        score_chunk(c, True)
        return carry

    lax.fori_loop(0, n_full, full_body, 0)
    lax.fori_loop(n_full, nch, edge_body, 0)

    def rows16(x):
        return jnp.broadcast_to(x, (PACKED_SUBLANES, qb)).astype(I16)

    def count16(src, ind, *args):
        def body(c, accs):
            v = src[c]
            accs = list(accs)
            for i in range(kc // PACKED_SUBLANES):
                j = i % len(accs)
                accs[j] = accs[j] + ind(v[i * PACKED_SUBLANES:(i + 1) * PACKED_SUBLANES], *args)
            return tuple(accs)

        zero = jnp.zeros((PACKED_SUBLANES, qb), I16)
        accs = lax.fori_loop(0, nch, body, (zero, zero, zero, zero))
        acc = (accs[0] + accs[1]) + (accs[2] + accs[3])
        return jnp.sum(acc.astype(I32), axis=0, keepdims=True)

    one16, zero16 = jnp.int16(1), jnp.int16(0)

    def digit_search(src, need):
        def bit_body(i, d_u):
            cand_u = d_u | jnp.left_shift(jnp.int32(1), 15 - i)
            cnt = count16(src, lambda v, c: jnp.where(v >= c, one16, zero16), rows16(cand_u - HALF_RANGE))
            return jnp.where(cnt >= need, cand_u, d_u)

        return lax.fori_loop(0, 16, bit_body, jnp.zeros((1, qb), I32)) - HALF_RANGE

    t_hi = digit_search(hi_scr, n_sel)
    t_hi16 = rows16(t_hi)
    cnt_hi_gt = count16(hi_scr, lambda v, c: jnp.where(v > c, one16, zero16), t_hi16)

    def low_digit_body(c, carry):
        hi, lo = hi_scr[c], lo_scr[c]
        out = []
        for i in range(kc // PACKED_SUBLANES):
            rows = slice(i * PACKED_SUBLANES, (i + 1) * PACKED_SUBLANES)
            out.append(jnp.where(hi[rows] == t_hi16, lo[rows], jnp.int16(-HALF_RANGE)))
        lo_scr[c] = jnp.concatenate(out, axis=0)
        return carry

    lax.fori_loop(0, nch, low_digit_body, 0)
    t_lo = digit_search(lo_scr, n_sel - cnt_hi_gt)
    t_lo16 = rows16(t_lo)
    cnt_gt = cnt_hi_gt + count16(lo_scr, lambda v, c: jnp.where(v > c, one16, zero16), t_lo16)
    cnt_ge = cnt_hi_gt + count16(lo_scr, lambda v, c: jnp.where(v >= c, one16, zero16), t_lo16)
    t = t_hi * (2 * HALF_RANGE) + (t_lo + HALF_RANGE)
    need = n_sel - cnt_gt
    live = t != jnp.int32(INT_MIN)
    excess = jnp.where(jnp.logical_and(live, cnt_ge > n_sel), 1.0, 0.0)
    jc_scr[...] = jnp.broadcast_to(jnp.where(live, jnp.int32(2 ** 30), jnp.int32(-1)), jc_scr.shape)

    @pl.when(jnp.max(excess) > 0.0)
    def _():
        def mark_body(c, carry):
            hi, lo = hi_scr[c], lo_scr[c]
            out = []
            for i in range(kc // PACKED_SUBLANES):
                rows = slice(i * PACKED_SUBLANES, (i + 1) * PACKED_SUBLANES)
                out.append(jnp.where(hi[rows] == t_hi16, jnp.where(lo[rows] == t_lo16, one16, zero16), zero16))
            lo_scr[c] = jnp.concatenate(out, axis=0)
            return carry

        lax.fori_loop(0, nch, mark_body, 0)
        row16 = lax.broadcasted_iota(I32, (PACKED_SUBLANES, qb), 0).astype(I16)

        def ties_below(cand):
            def body(c, accs):
                v = lo_scr[c]
                rel = rows16(cand - c * kc)
                accs = list(accs)
                for i in range(kc // PACKED_SUBLANES):
                    j = i % len(accs)
                    below = (row16 + jnp.int16(i * PACKED_SUBLANES)) < rel
                    accs[j] = accs[j] + jnp.where(below, v[i * PACKED_SUBLANES:(i + 1) * PACKED_SUBLANES], zero16)
                return tuple(accs)

            zero = jnp.zeros((PACKED_SUBLANES, qb), I16)
            accs = lax.fori_loop(0, nch, body, (zero, zero, zero, zero))
            acc = (accs[0] + accs[1]) + (accs[2] + accs[3])
            return jnp.sum(acc.astype(I32), axis=0, keepdims=True)

        jcut = _tie_cutoff(ties_below, need, (1, qb), n_bits)
        jc_scr[...] = jnp.broadcast_to(jnp.where(live, jcut, jnp.int32(-1)), jc_scr.shape)

    jcut = jc_scr[0:1, :]

    def logit_step(g, qg, c):
        s = _dot_nt(key_chunk(kb_ref, c), qg)
        if g == 0:
            m_add = _mask_add(key_scr[c], key_index(c), t, jcut)
            hi_scr[c] = pltpu.bitcast(m_add.astype(jnp.bfloat16), I16)
        else:
            m_add = pltpu.bitcast(hi_scr[c], jnp.bfloat16).astype(F32)
        for i in range(hpg):
            cols = slice(i * qb, (i + 1) * qb)
            si = s[:, cols] + m_add
            s_scr[g][c, :, cols] = si
            mx_scr[g][:, cols] = jnp.maximum(mx_scr[g][:, cols], _fold_rows(si, SUBLANES, jnp.maximum))

    def value_step(g, m, c):
        p = jnp.exp2(s_scr[g][c] - m)
        ls_scr[...] += _fold_rows(p, SUBLANES, jnp.add)
        acc_scr[...] += _dot(vbt_ref[0, c], p.astype(MXU_DTYPE))

    def chunk_loop(step):
        def body(i, carry):
            step(2 * i)
            step(2 * i + 1)
            return carry
        lax.fori_loop(0, nch // 2, body, 0)

        @pl.when(nch % 2 == 1)
        def _():
            step(nch - 1)

    qgs = [jnp.concatenate([qp_ref[0, :, (g * hpg + i) * LANES:(g * hpg + i + 1) * LANES] for i in range(hpg)],
                           axis=0) for g in range(N_KV)]
    for mx in mx_scr:
        mx[...] = jnp.full(mx.shape, NEG_INF, F32)
    chunk_loop(lambda c: logit_step(0, qgs[0], c))
    for g in range(N_KV):
        m = jnp.max(mx_scr[g][...], axis=0, keepdims=True)
        ls_scr[...] = jnp.zeros(ls_scr.shape, F32)
        acc_scr[...] = jnp.zeros(acc_scr.shape, F32)
        if g + 1 < N_KV:
            def both(c, g=g, m=m):
                value_step(g, m, c)
                logit_step(g + 1, qgs[g + 1], c)
            chunk_loop(both)
        else:
            chunk_loop(lambda c, g=g, m=m: value_step(g, m, c))
        o_t = acc_scr[...] / jnp.sum(ls_scr[...], axis=0, keepdims=True)
        feats = slice(g * HEAD_DIM, (g + 1) * HEAD_DIM)
        for pair in range(hpg // 2):
            even = o_t[feats, (2 * pair) * qb:(2 * pair + 1) * qb]
            odd = o_t[feats, (2 * pair + 1) * qb:(2 * pair + 2) * qb]
            tile = g * (hpg // 2) + pair
            o_ref[0, :, tile * LANES:(tile + 1) * LANES] = jnp.concatenate([even, odd], axis=0).T.astype(o_ref.dtype)


def _attn_prompt(qp, qip, wit, ki2, kb, vbt, n_sel):
    b, s, _ = qp.shape
    kc = vbt.shape[3]
    qb = ATT_QUERIES if s % ATT_QUERIES == 0 else Q_BLOCK
    nq = s // qb
    n_bits = max(1, (s - 1).bit_length())
    blk = lambda i, j: (i, j, 0)
    seq = lambda i, j: (i, 0, 0)
    n = (N_HEADS // N_KV) * qb
    return pl.pallas_call(
        functools.partial(_attn_prompt_kernel, n_sel=n_sel, n_bits=n_bits, kc=kc, qb=qb),
        out_shape=jax.ShapeDtypeStruct((b, s, N_HEADS * HEAD_DIM), MXU_DTYPE),
        grid=(b, nq),
        in_specs=[pl.BlockSpec((1, qb, qp.shape[2]), blk),
                  pl.BlockSpec((1, qb, qip.shape[2]), blk),
                  pl.BlockSpec((SUBLANES, qb), lambda i, j: (0, i * nq + j)),
                  pl.BlockSpec((1, s, LANES), seq),
                  pl.BlockSpec((1, s, LANES), seq),
                  pl.BlockSpec((1, s // kc, LANES, kc), lambda i, j: (i, 0, 0, 0))],
        out_specs=pl.BlockSpec((1, qb, N_HEADS * HEAD_DIM), blk),
        scratch_shapes=[pltpu.VMEM((s // kc, kc, qb), I32),
                        pltpu.VMEM((s // kc, kc, qb), I16),
                        pltpu.VMEM((s // kc, kc, qb), I16),
                        pltpu.VMEM((s // kc, kc, n), F32),
                        pltpu.VMEM((s // kc, kc, n), F32),
                        pltpu.VMEM((SUBLANES, n), F32),
                        pltpu.VMEM((SUBLANES, n), F32),
                        pltpu.VMEM((SUBLANES, n), F32),
                        pltpu.VMEM((LANES, n), F32),
                        pltpu.VMEM((SUBLANES, qb), I32)],
        compiler_params=_cparams(2),
        name="attn_prompt",
    )(qp, qip, wit, ki2, kb, vbt)


def _out_kernel(x_ref, yl_ref, ya_ref, mod_ref, g_ref, wo_ref, o_ref, *, lw):
    y = _dot(yl_ref[...], wo_ref[:lw, :]) + _dot(ya_ref[...], wo_ref[lw:, :])
    o_ref[...] = x_ref[...] + mod_ref[0, 5] * _rms(y, g_ref[3:4, :])


def _out_proj(x, y_lru, y_att, mod4, g, wo):
    t, d = x.shape
    lw = y_lru.shape[1]
    tm = _token_tile(t)
    n_tiles = t // tm
    tok = lambda i: (i, 0)
    return pl.pallas_call(
        functools.partial(_out_kernel, lw=lw),
        out_shape=jax.ShapeDtypeStruct((t, d), F32),
        grid=(n_tiles,),
        in_specs=[pl.BlockSpec((tm, d), tok), pl.BlockSpec((tm, lw), tok),
                  pl.BlockSpec((tm, y_att.shape[1]), tok), _mod_spec(mod4, n_tiles),
                  pl.BlockSpec(g.shape, lambda i: (0, 0)), pl.BlockSpec(wo.shape, lambda i: (0, 0))],
        out_specs=pl.BlockSpec((tm, d), tok),
        compiler_params=_cparams(1),
        name="out_proj",
    )(x, y_lru, y_att, mod4, g, wo)


def _attn_sample_kernel(pt_ref, qs_ref, qis_ref, wcol_ref, kin_ref, kn_ref, vn_ref, ckidx_ref, ck_ref, cv_ref,
                        o_ref, kidx_buf, k_buf, v_buf, key_scr, madd_scr, s_scr, jc_scr, sems,
                        *, layer, n_pages, page, ds, n_sel, n_bits, kc):
    bi = pl.program_id(0)
    past = n_pages * page
    total = past + LANES
    n_chunks = past // kc
    slot = bi % 2

    streams = ((ckidx_ref, kidx_buf), (ck_ref, k_buf), (cv_ref, v_buf))

    def page_copy(which, seq, slot_, p):
        src, dst = streams[which]
        cols = pl.ds(pl.multiple_of(p * page, page), page)
        return pltpu.make_async_copy(src.at[layer, pt_ref[seq, p]], dst.at[slot_, :, cols], sems.at[slot_, which])

    def fetch(seq, slot_):
        def body(p, carry):
            for which in range(len(streams)):
                page_copy(which, seq, slot_, p).start()
            return carry
        lax.fori_loop(0, n_pages, body, 0)

    def wait_stream(which):
        dst = streams[which][1].at[slot, :, 0:past]
        pltpu.make_async_copy(dst, dst, sems.at[slot, which]).wait()

    @pl.when(bi == 0)
    def _():
        fetch(bi, slot)

    @pl.when(bi + 1 < pl.num_programs(0))
    def _():
        fetch(bi + 1, 1 - slot)

    kidx_buf[slot, :, past:total] = kin_ref[0]
    k_buf[slot, :, past:total] = kn_ref[0]
    v_buf[slot, :, past:total] = vn_ref[0]
    key_scr[...] = jnp.full(key_scr.shape, INT_MIN, I32)

    qis = qis_ref[0]
    wcol = wcol_ref[0][:, 0:1]
    wait_stream(0)

    def scores(lo_col, width):
        d = _dot(qis, kidx_buf[slot, :, lo_col:lo_col + width].astype(MXU_DTYPE))
        wgt = jnp.maximum(d, 0.0) * wcol
        sc = wgt[0:ds]
        for h in range(1, IDX_HEADS):
            sc = sc + wgt[h * ds:(h + 1) * ds]
        return sc

    top = SUBLANES // 2
    half = past // 2
    hw = key_scr.shape[1]

    def place(lo_col):
        return (0, lo_col) if lo_col < half else (top, lo_col - half)

    for c in range(n_chunks):
        r0, c0 = place(c * kc)
        key_scr[r0:r0 + ds, c0:c0 + kc] = _score_key(scores(c * kc, kc))
    sc_new = scores(past, LANES)
    lane_new = lax.broadcasted_iota(I32, (ds, LANES), 1)
    row_new = lax.broadcasted_iota(I32, (ds, LANES), 0)
    r0, c0 = place(past)
    key_scr[r0:r0 + ds, c0:c0 + LANES] = jnp.where(lane_new > row_new, jnp.int32(INT_MIN), _score_key(sc_new))

    rows = key_scr.shape[0]
    lane = lax.broadcasted_iota(I32, (rows, LANES), 1)
    col0 = lane + jnp.where(lax.broadcasted_iota(I32, (rows, LANES), 0) >= top, half, 0)

    def count(ind, *row_args):
        args = [jnp.broadcast_to(a, (rows, LANES)) for a in row_args]
        accs = None
        for i in range(hw // LANES):
            vals = ind(key_scr[:, i * LANES:(i + 1) * LANES], col0 + i * LANES, *args)
            accs = vals if accs is None else tuple(a + v for a, v in zip(accs, vals))
        return tuple(jnp.sum(a + pltpu.roll(a, top, 0), axis=1, keepdims=True) for a in accs)

    def count_ge3(c1, c2, c3):
        return count(lambda k, col, b1, b2, b3: (jnp.where(k >= b1, 1.0, 0.0), jnp.where(k >= b2, 1.0, 0.0),
                                                 jnp.where(k >= b3, 1.0, 0.0)), c1, c2, c3)

    t = _kth_largest_key_radix4(count_ge3, (rows, 1), n_sel)
    cnt_gt, cnt_ge = count(lambda k, col, tb: (jnp.where(k > tb, 1.0, 0.0), jnp.where(k >= tb, 1.0, 0.0)), t)
    need = n_sel - cnt_gt
    live = t != jnp.int32(INT_MIN)
    excess = jnp.where(jnp.logical_and(live, cnt_ge > n_sel), 1.0, 0.0)
    jc_scr[...] = jnp.broadcast_to(jnp.where(live, jnp.int32(2 ** 30), jnp.int32(-1)), jc_scr.shape)

    @pl.when(jnp.max(excess) > 0.0)
    def _():
        def ties_below(cand):
            return count(lambda k, col, tb, cb: (jnp.where(k == tb, jnp.where(col < cb, 1.0, 0.0), 0.0),), t, cand)[0]

        jcut = _tie_cutoff(ties_below, need, (rows, 1), n_bits)
        jc_scr[...] = jnp.broadcast_to(jnp.where(live, jcut, jnp.int32(-1)), jc_scr.shape)

    tb, jb = jnp.broadcast_to(t, (rows, LANES)), jc_scr[...]
    for i in range(hw // LANES):
        cols = slice(i * LANES, (i + 1) * LANES)
        madd_scr[:, cols] = _mask_add(key_scr[:, cols], col0 + i * LANES, tb, jb)

    qs = qs_ref[0]
    nrow = qs.shape[0]

    def row_mask(lo_col, width):
        r0, c0 = place(lo_col)
        return jnp.concatenate(
            [jnp.broadcast_to(madd_scr[r0 + tt:r0 + tt + 1, c0:c0 + width], (N_HEADS, width)) for tt in range(ds)],
            axis=0)

    spans = [(c * kc, kc) for c in range(n_chunks)] + [(past, LANES)]
    wait_stream(1)
    m = jnp.full((nrow, 1), NEG_INF, F32)
    for lo_col, width in spans:
        s = _dot(qs, k_buf[slot, :, lo_col:lo_col + width].astype(MXU_DTYPE)) + row_mask(lo_col, width)
        s_scr[:, lo_col:lo_col + width] = s
        m = jnp.maximum(m, jnp.max(s, axis=1, keepdims=True))
    wait_stream(2)
    l = jnp.zeros((nrow, 1), F32)
    acc = jnp.zeros((nrow, LANES), F32)
    for lo_col, width in spans:
        p = jnp.exp2(s_scr[:, lo_col:lo_col + width] - m)
        l = l + jnp.sum(p, axis=1, keepdims=True)
        acc = acc + _dot_nt(p.astype(MXU_DTYPE), v_buf[slot, :, lo_col:lo_col + width].astype(MXU_DTYPE))
    o_ref[0] = acc / l


def _attn_sample(page_table, qs, qis, wcol, ki_new, k_new, v_new, cache_kidx, cache_k2, cache_v2, layer, n_sel, ds):
    db, n_pages = page_table.shape
    page = cache_kidx.shape[3]
    past = n_pages * page
    total = past + LANES
    kc = 2048 if past % 4096 == 0 else page
    assert (past // kc) % 2 == 0 and 2 * ds <= SUBLANES
    n_bits = max(1, (total - 1).bit_length())
    per = lambda i, pt: (i, 0, 0)
    any_spec = pl.BlockSpec(memory_space=pl.ANY)
    grid_spec = pltpu.PrefetchScalarGridSpec(
        num_scalar_prefetch=1,
        grid=(db,),
        in_specs=[pl.BlockSpec((1,) + qs.shape[1:], per), pl.BlockSpec((1,) + qis.shape[1:], per),
                  pl.BlockSpec((1,) + wcol.shape[1:], per), pl.BlockSpec((1,) + ki_new.shape[1:], per),
                  pl.BlockSpec((1,) + k_new.shape[1:], per), pl.BlockSpec((1,) + v_new.shape[1:], per),
                  any_spec, any_spec, any_spec],
        out_specs=pl.BlockSpec((1, qs.shape[1], LANES), per),
        scratch_shapes=[pltpu.VMEM((2, IDX_DIM, total), F32),
                        pltpu.VMEM((2, LANES, total), F32),
                        pltpu.VMEM((2, LANES, total), F32),
                        pltpu.VMEM((SUBLANES, past // 2 + LANES), I32),
                        pltpu.VMEM((SUBLANES, past // 2 + LANES), F32),
                        pltpu.VMEM((qs.shape[1], total), F32),
                        pltpu.VMEM((SUBLANES, LANES), I32),
                        pltpu.SemaphoreType.DMA((2, 3))])
    return pl.pallas_call(
        functools.partial(_attn_sample_kernel, layer=layer, n_pages=n_pages, page=page, ds=ds, n_sel=n_sel,
                          n_bits=n_bits, kc=kc),
        out_shape=jax.ShapeDtypeStruct((db, qs.shape[1], LANES), F32),
        grid_spec=grid_spec,
        compiler_params=_cparams(1),
        name="attn_sample",
    )(page_table, qs, qis, wcol, ki_new, k_new, v_new, cache_kidx, cache_k2, cache_v2)


def _rope_tables(pos):
    half = ROPE_DIM // 2
    freqs = ROPE_THETA ** (-(jnp.arange(half, dtype=F32) / half))
    ang = pos.astype(F32)[:, None] * freqs[None, :]
    cos, sin = jnp.cos(ang), jnp.sin(ang)
    rest = HEAD_DIM - ROPE_DIM
    c64 = jnp.concatenate([cos, cos, jnp.ones((pos.shape[0], rest), F32)], axis=1)
    s64 = jnp.concatenate([-sin, sin, jnp.zeros((pos.shape[0], rest), F32)], axis=1)
    return jnp.tile(c64, (1, LANES // HEAD_DIM)), jnp.tile(s64, (1, LANES // HEAD_DIM))


def _block_diag(w):
    nb, n, _ = w.shape
    eye = jnp.eye(nb, dtype=w.dtype)
    return (eye[:, None, :, None] * w[:, :, None, :]).reshape(nb * n, nb * n)


def kernel(x_prompt, x_sample, c_prompt, c_sample, cache_k, cache_v, cache_kidx, state_h, state_conv, page_table,
           w_ada, b_ada, norm_g, ffn_w_gu, ffn_w_down, w_in, conv_w, conv_b, lru_wa, lru_ba, lru_wx, lru_bx,
           lru_lambda, w_out):
    b, s, d = x_prompt.shape
    db, ds, _ = x_sample.shape
    depth = w_ada.shape[0]
    lw = lru_lambda.shape[1]
    n_pool, page = cache_k.shape[1], cache_k.shape[2]
    past = page_table.shape[1] * page
    kv_w = N_KV * HEAD_DIM
    tm_p = _token_tile(b * s)
    assert kv_w == LANES and 2 * IDX_DIM == LANES and s % tm_p == 0 and tm_p % Q_BLOCK == 0 and page % LANES == 0

    cos_p, sin_p = _rope_tables(jnp.arange(s, dtype=I32))
    cos_s, sin_s = _rope_tables(past + jnp.arange(ds, dtype=I32))
    cos_s, sin_s = jnp.tile(cos_s, (db, 1)), jnp.tile(sin_s, (db, 1))

    n_c = b + db
    c_all = jnp.concatenate([c_prompt, c_sample, jnp.zeros((-n_c % SUBLANES, d), F32)], axis=0)
    cache_kidx_t = cache_kidx.transpose(0, 1, 3, 2)
    cache_k_t = cache_k.transpose(0, 1, 3, 4, 2).reshape(depth, n_pool, kv_w, page)
    cache_v_t = cache_v.transpose(0, 1, 3, 4, 2).reshape(depth, n_pool, kv_w, page)

    def new_cols(a):
        a = a.reshape(db, ds, a.shape[-1]).transpose(0, 2, 1)
        return jnp.pad(a, ((0, 0), (0, 0), (0, LANES - ds)))

    xp = x_prompt.reshape(b * s, d)
    xs = x_sample.reshape(db * ds, d)
    n_sel_p = min(TOP_K, s // 4)
    n_sel_s = min(TOP_K, (past + ds) // 4)
    q_w = N_HEADS * HEAD_DIM
    qi_w = IDX_HEADS * IDX_DIM
    col_ki = 2 * lw + q_w + 2 * kv_w + qi_w

    outs = {name: [] for name in ("kp", "vp", "kip", "hp", "cp", "ks", "vs", "kis", "hs", "cs")}
    for l in range(depth):
        mod = _ada_mod(c_all, w_ada[l], b_ada[l])
        mod_p = mod[:b].reshape(b, N_MOD, 1, d)
        mod_s = jnp.repeat(mod[b:n_c].reshape(db, N_MOD, d), ds, axis=0).transpose(1, 0, 2)[None]
        g = norm_g[l]
        wgu = ffn_w_gu[l].astype(MXU_DTYPE)
        wdn = ffn_w_down[l].astype(MXU_DTYPE)
        w_ki = w_in[l][:, col_ki:col_ki + IDX_DIM]
        w_pad = jnp.concatenate(
            [w_in[l][:, :col_ki], w_ki, w_ki, w_in[l][:, col_ki + IDX_DIM:],
             jnp.zeros((d, LANES - IDX_HEADS), F32)], axis=1).astype(MXU_DTYPE)
        wo = w_out[l].astype(MXU_DTYPE)
        wa_bd = _block_diag(lru_wa[l]).astype(MXU_DTYPE)
        wx_bd = _block_diag(lru_wx[l]).astype(MXU_DTYPE)
        lru_args = (conv_w[l], conv_b[l].reshape(1, lw), wa_bd, lru_ba[l].reshape(1, lw), wx_bd,
                    lru_bx[l].reshape(1, lw), lru_lambda[l].reshape(1, lw))

        xp = _ffn(xp, mod_p, g, wgu[0], wdn[0], 0, 0)
        xl, gl, qp, k, v, kb, vbt, qip, ki, ki2, _, wit = _proj(xp, mod_p, g, w_pad, cos_p, sin_p, lw, s // tm_p)
        r3 = lambda a: a.reshape(b, s, a.shape[-1])
        y_lru, h_t, nbuf = _lru_prompt(r3(xl), r3(gl), jnp.zeros((b, CONV_W - 1, lw), F32), jnp.zeros((b, lw), F32),
                                       *lru_args)
        y_att = _attn_prompt(r3(qp), r3(qip), wit, r3(ki2), r3(kb), vbt.reshape(b, s // tm_p, LANES, tm_p), n_sel_p)
        xp = _out_proj(xp, y_lru.reshape(b * s, lw), y_att.reshape(b * s, q_w), mod_p, g, wo)
        xp = _ffn(xp, mod_p, g, wgu[1], wdn[1], 6, 4)
        outs["kp"].append(k.reshape(b, s, N_KV, HEAD_DIM))
        outs["vp"].append(v.reshape(b, s, N_KV, HEAD_DIM))
        outs["kip"].append(ki.reshape(b, s, IDX_DIM))
        outs["hp"].append(h_t.reshape(b, lw))
        outs["cp"].append(nbuf)

        xs = _ffn(xs, mod_s, g, wgu[0], wdn[0], 0, 0)
        xl, gl, qp, k, v, _, _, qip, ki, _, wi, _ = _proj(xs, mod_s, g, w_pad, cos_s, sin_s, lw, 1)
        tmaj = lambda a: a.reshape(db, ds, a.shape[-1]).transpose(1, 0, 2)
        y_lru, h_t, nbuf = _lru_sample(tmaj(xl), tmaj(gl), state_conv[l].transpose(1, 0, 2), state_h[l], *lru_args)
        qs = qp.reshape(db, ds * N_HEADS, LANES)
        qis = qip.reshape(db, ds, IDX_HEADS, 2, IDX_DIM)
        qis = jnp.stack([qis[:, :, h, h % 2] for h in range(IDX_HEADS)], axis=1).reshape(db, IDX_HEADS * ds, IDX_DIM)
        wcol = wi.reshape(db, ds, LANES)[:, :, :IDX_HEADS].transpose(0, 2, 1).reshape(db, IDX_HEADS * ds, 1)
        wcol = jnp.broadcast_to(wcol, (db, IDX_HEADS * ds, LANES))
        o_s = _attn_sample(page_table, qs, qis, wcol, new_cols(ki), new_cols(k), new_cols(v),
                           cache_kidx_t, cache_k_t, cache_v_t, l, n_sel_s, ds)
        o_s = o_s.reshape(db, ds, N_HEADS, N_KV, HEAD_DIM)
        hpg = N_HEADS // N_KV
        y_att = jnp.stack([o_s[:, :, h, h // hpg] for h in range(N_HEADS)], axis=2).reshape(db * ds, q_w)
        xs = _out_proj(xs, y_lru.transpose(1, 0, 2).reshape(db * ds, lw), y_att.astype(MXU_DTYPE), mod_s, g, wo)
        xs = _ffn(xs, mod_s, g, wgu[1], wdn[1], 6, 4)
        outs["ks"].append(k.reshape(db, ds, N_KV, HEAD_DIM))
        outs["vs"].append(v.reshape(db, ds, N_KV, HEAD_DIM))
        outs["kis"].append(ki.reshape(db, ds, IDX_DIM))
        outs["hs"].append(h_t)
        outs["cs"].append(nbuf.transpose(1, 0, 2))

    st = lambda name: jnp.stack(outs[name])
    return (xp.reshape(b, s, d), xs.reshape(db, ds, d), st("kp"), st("vp"), st("kip"), st("hp"), st("cp"),
            st("ks"), st("vs"), st("kis"), st("hs"), st("cs"))
```

```python
import functools
import math

import jax
import jax.numpy as jnp
from jax import lax
from jax.experimental import pallas as pl
from jax.experimental.pallas import tpu as pltpu

F32 = jnp.float32
I32 = jnp.int32
I16 = jnp.int16
MXU_DTYPE = jnp.bfloat16

LRU_BLOCKS = 8
CONV_W = 4
LRU_C = 8.0
N_HEADS = 8
HEAD_DIM = 64
N_KV = 2
ROPE_DIM = HEAD_DIM // 4
ROPE_THETA = 500000.0
IDX_HEADS = 4
IDX_DIM = 64
TOP_K = 256
Q_BLOCK = 128
N_MOD = 9
RMS_EPS = 1e-6
LOG2E = math.log2(math.e)

LANES = 128
SUBLANES = 8
PACKED_SUBLANES = 16
TOKEN_TILE = 512
ATT_QUERIES = 256
VMEM_LIMIT = 56 * 1024 * 1024
INT_MIN = -2147483648
KEY_NEG_INF = -2139095041
NEG_INF = float("-inf")


def _cparams(n_axes):
    return pltpu.CompilerParams(dimension_semantics=("arbitrary",) * n_axes,
                                vmem_limit_bytes=VMEM_LIMIT)


def _rms(x, g):
    ms = jnp.mean(x * x, axis=-1, keepdims=True)
    return x * lax.rsqrt(ms + RMS_EPS) * g


def _dot(a, b):
    return jnp.dot(a, b, preferred_element_type=F32)


def _dot_nt(a, b):
    return lax.dot_general(a, b, (((1,), (1,)), ((), ())), preferred_element_type=F32)


def _mod_kernel(c_ref, w_ref, b_ref, o_ref):
    o_ref[...] = jnp.dot(c_ref[...], w_ref[...], preferred_element_type=F32,
                         precision=lax.Precision.HIGHEST) + b_ref[...]


def _pick(arr, *lead):
    rest = arr.shape[len(lead):]
    return pl.BlockSpec((None,) * len(lead) + rest, lambda *_: lead + (0,) * len(rest))


def _ada_mod(c_all, w_all, b_all, layer):
    rows, d = c_all.shape
    n = w_all.shape[2]
    tn = d
    return pl.pallas_call(
        _mod_kernel,
        out_shape=jax.ShapeDtypeStruct((rows, n), F32),
        grid=(n // tn,),
        in_specs=[pl.BlockSpec((rows, d), lambda i: (0, 0)),
                  pl.BlockSpec((None, d, tn), lambda i: (layer, 0, i)),
                  pl.BlockSpec((None, 1, tn), lambda i: (layer, 0, i))],
        out_specs=pl.BlockSpec((rows, tn), lambda i: (0, i)),
        compiler_params=_cparams(1),
        name="ada_mod",
    )(c_all, w_all, b_all.reshape(b_all.shape[0], 1, n))


def _ffn_block(x, mod_ref, g_ref, wgu_ref, wdn_ref, m0, gi, d_ff):
    sh, sc, gt = mod_ref[0, m0], mod_ref[0, m0 + 1], mod_ref[0, m0 + 2]
    h = (_rms(x, g_ref[gi:gi + 1, :]) * (1.0 + sc) + sh).astype(MXU_DTYPE)
    g = _dot(h, wgu_ref[:, :d_ff])
    u = _dot(h, wgu_ref[:, d_ff:])
    a = (g * jax.nn.sigmoid(g) * u).astype(MXU_DTYPE)
    f = _dot(a, wdn_ref[...])
    return x + 0.5 * gt * _rms(f, g_ref[gi + 1:gi + 2, :])


def _ffn_kernel(x_ref, mod_ref, g_ref, wgu_ref, wdn_ref, o_ref, *, m0, gi, d_ff):
    o_ref[...] = _ffn_block(x_ref[...], mod_ref, g_ref, wgu_ref, wdn_ref, m0, gi, d_ff)


def _out_ffn_kernel(x_ref, yl_ref, ya_ref, mod_ref, g_ref, wo_ref, wgu_ref, wdn_ref, o_ref, *, lw, d_ff):
    y = _dot(yl_ref[...], wo_ref[:lw, :]) + _dot(ya_ref[...], wo_ref[lw:, :])
    x = x_ref[...] + mod_ref[0, 5] * _rms(y, g_ref[3:4, :])
    o_ref[...] = _ffn_block(x, mod_ref, g_ref, wgu_ref, wdn_ref, 6, 4, d_ff)


def _token_tile(t):
    return TOKEN_TILE if t % TOKEN_TILE == 0 else t


def _mod_spec(mod4, n_tiles):
    nb, _, r, d = mod4.shape
    per = n_tiles // nb
    return pl.BlockSpec((1, N_MOD, r, d), lambda i: (i // per, 0, 0, 0))


def _ffn(x, mod4, g_all, wgu_all, wdn_all, layer, which, m0, gi):
    t, d = x.shape
    tm = _token_tile(t)
    n_tiles = t // tm
    return pl.pallas_call(
        functools.partial(_ffn_kernel, m0=m0, gi=gi, d_ff=wdn_all.shape[-2]),
        out_shape=jax.ShapeDtypeStruct((t, d), F32),
        grid=(n_tiles,),
        in_specs=[pl.BlockSpec((tm, d), lambda i: (i, 0)),
                  _mod_spec(mod4, n_tiles),
                  _pick(g_all, layer), _pick(wgu_all, layer, which), _pick(wdn_all, layer, which)],
        out_specs=pl.BlockSpec((tm, d), lambda i: (i, 0)),
        compiler_params=_cparams(1),
        name="ffn",
    )(x, mod4, g_all, wgu_all, wdn_all)


def _out_ffn(x, y_lru, y_att, mod4, g_all, wo_all, wgu_all, wdn_all, layer):
    t, d = x.shape
    lw = y_lru.shape[1]
    tm = _token_tile(t)
    n_tiles = t // tm
    tok = lambda i: (i, 0)
    return pl.pallas_call(
        functools.partial(_out_ffn_kernel, lw=lw, d_ff=wdn_all.shape[-2]),
        out_shape=jax.ShapeDtypeStruct((t, d), F32),
        grid=(n_tiles,),
        in_specs=[pl.BlockSpec((tm, d), tok), pl.BlockSpec((tm, lw), tok),
                  pl.BlockSpec((tm, y_att.shape[1]), tok), _mod_spec(mod4, n_tiles),
                  _pick(g_all, layer), _pick(wo_all, layer), _pick(wgu_all, layer, 1), _pick(wdn_all, layer, 1)],
        out_specs=pl.BlockSpec((tm, d), tok),
        compiler_params=_cparams(1),
        name="out_ffn",
    )(x, y_lru, y_att, mod4, g_all, wo_all, wgu_all, wdn_all)


def _rope(x, cos, sin):
    lane = lax.broadcasted_iota(I32, x.shape, 1)
    first = (lane & (HEAD_DIM - 1)) < (ROPE_DIM // 2)
    partner = jnp.where(first, pltpu.roll(x, LANES - ROPE_DIM // 2, 1), pltpu.roll(x, ROPE_DIM // 2, 1))
    return x * cos + partner * sin


def _proj_kernel(x_ref, mod_ref, g_ref, w_ref, cos_ref, sin_ref,
                 xl_ref, gl_ref, qp_ref, kt_ref, vt_ref, kb_ref, vbt_ref, qip_ref, kit_ref, ki2_ref, wi_ref, wit_ref,
                 *, lw):
    x = x_ref[...]
    h = (_rms(x, g_ref[2:3, :]) * (1.0 + mod_ref[0, 4]) + mod_ref[0, 3]).astype(MXU_DTYPE)
    p = _dot(h, w_ref[...])
    cos, sin = cos_ref[...], sin_ref[...]
    lane = lax.broadcasted_iota(I32, cos.shape, 1)
    lo = lane < HEAD_DIM

    o = 0
    xl_ref[...] = p[:, o:o + lw]
    o += lw
    gl_ref[...] = p[:, o:o + lw]
    o += lw
    q_scale = HEAD_DIM ** -0.5 * LOG2E
    for pair in range(N_HEADS // 2):
        qr = _rope(p[:, o + pair * LANES:o + (pair + 1) * LANES], cos, sin) * q_scale
        qr_sw = pltpu.roll(qr, HEAD_DIM, 1)
        for half in range(2):
            hd = 2 * pair + half
            grp = hd // (N_HEADS // N_KV)
            src = qr if half == grp else qr_sw
            slab = jnp.where(lo if grp == 0 else jnp.logical_not(lo), src, 0.0)
            qp_ref[:, hd * LANES:(hd + 1) * LANES] = slab.astype(qp_ref.dtype)
    o += N_HEADS * HEAD_DIM
    k = _rope(p[:, o:o + LANES], cos, sin)
    kt_ref[0] = k.T
    kb_ref[...] = k.astype(kb_ref.dtype)
    o += LANES
    v_t = p[:, o:o + LANES].T
    vt_ref[0] = v_t
    vbt_ref[0] = v_t.astype(vbt_ref.dtype)
    o += LANES
    for pair in range(IDX_HEADS // 2):
        qi = _rope(p[:, o + pair * LANES:o + (pair + 1) * LANES], cos, sin)
        for half in range(2):
            hd = 2 * pair + half
            slab = jnp.where(lo if half == 0 else jnp.logical_not(lo), qi, 0.0)
            qip_ref[:, hd * LANES:(hd + 1) * LANES] = slab.astype(qip_ref.dtype)
    o += IDX_HEADS * IDX_DIM
    ki2 = _rope(p[:, o:o + LANES], cos, sin)
    kit_ref[0] = ki2.T[:IDX_DIM, :]
    ki2_ref[...] = ki2.astype(ki2_ref.dtype)
    o += LANES
    wi = p[:, o:o + LANES] * (IDX_DIM ** -0.5 * IDX_HEADS ** -0.5)
    wi_ref[...] = wi
    wit_ref[...] = wi.T[:SUBLANES, :]


def _proj(x, mod4, g_all, w_pad_all, layer, cos, sin, lw, pos_tiles):
    t, d = x.shape
    tm = _token_tile(t)
    n_tiles = t // tm
    n_seq = n_tiles // pos_tiles
    tok = lambda i: (i, 0)
    pos = lambda i: (i % pos_tiles, 0)
    by_seq = lambda i: (i // pos_tiles, 0, i % pos_tiles)

    def rows(width, dtype):
        return jax.ShapeDtypeStruct((t, width), dtype), pl.BlockSpec((tm, width), tok)

    def feature_major(width):
        return (jax.ShapeDtypeStruct((n_seq, width, pos_tiles * tm), F32), pl.BlockSpec((1, width, tm), by_seq))

    outs = [rows(lw, F32), rows(lw, F32), rows(N_HEADS * LANES, MXU_DTYPE),
            feature_major(LANES), feature_major(LANES), rows(LANES, MXU_DTYPE),
            (jax.ShapeDtypeStruct((n_tiles, LANES, tm), MXU_DTYPE), pl.BlockSpec((1, LANES, tm), lambda i: (i, 0, 0))),
            rows(IDX_HEADS * LANES, MXU_DTYPE), feature_major(IDX_DIM), rows(LANES, MXU_DTYPE), rows(LANES, F32),
            (jax.ShapeDtypeStruct((SUBLANES, t), F32), pl.BlockSpec((SUBLANES, tm), lambda i: (0, i)))]
    return pl.pallas_call(
        functools.partial(_proj_kernel, lw=lw),
        out_shape=[o[0] for o in outs],
        grid=(n_tiles,),
        in_specs=[pl.BlockSpec((tm, d), tok),
                  _mod_spec(mod4, n_tiles),
                  _pick(g_all, layer), _pick(w_pad_all, layer),
                  pl.BlockSpec((tm, LANES), pos),
                  pl.BlockSpec((tm, LANES), pos)],
        out_specs=[o[1] for o in outs],
        compiler_params=_cparams(1),
        name="mixer_proj",
    )(x, mod4, g_all, w_pad_all, cos, sin)


def _gelu_tanh(x):
    return x * (0.5 * (1.0 + jnp.tanh(0.7978845608028654 * (x + 0.044715 * (x * x * x)))))


def _lru_coeffs(xc, wa_ref, ba_ref, wx_ref, bx_ref, lam_ref):
    xcb = xc.astype(MXU_DTYPE)
    r = jax.nn.sigmoid(_dot(xcb, wa_ref[...]) + ba_ref[...])
    i = jax.nn.sigmoid(_dot(xcb, wx_ref[...]) + bx_ref[...])
    nl = -lam_ref[...]
    softplus = jnp.maximum(nl, 0.0) + jnp.log1p(jnp.exp(-jnp.abs(nl)))
    log_a = (-LRU_C) * r * softplus
    a = jnp.exp(log_a)
    one_minus_a2 = -jnp.tanh(log_a) * (a * a + 1.0)
    return a, jnp.sqrt(one_minus_a2) * (i * xc)


def _lru_prompt_kernel(xl_ref, gl_ref, buf_ref, h0_ref, cw_ref, cb_ref, wa_ref, ba_ref, wx_ref, bx_ref, lam_ref,
                       y_ref, ht_ref, nb_ref, xs_scr, a_scr, b_scr, h_scr, *, ts):
    tj = pl.program_id(1)
    pad = SUBLANES
    ng = ts // SUBLANES

    @pl.when(tj == 0)
    def _():
        xs_scr[0:pad, :] = jnp.zeros((pad, xs_scr.shape[1]), F32)
        xs_scr[pad - (CONV_W - 1):pad, :] = buf_ref[0]
        h_scr[...] = h0_ref[0]

    x = xl_ref[0]
    xs_scr[pad:pad + ts, :] = x
    xc = cb_ref[...] + x * cw_ref[CONV_W - 1:CONV_W, :]
    for j in range(1, CONV_W):
        xc = xc + xs_scr[pad - j:pad - j + ts, :] * cw_ref[CONV_W - 1 - j:CONV_W - j, :]

    a, b = _lru_coeffs(xc, wa_ref, ba_ref, wx_ref, bx_ref, lam_ref)
    lw = a.shape[1]
    a = a.reshape(ng, SUBLANES, lw)
    b = b.reshape(ng, SUBLANES, lw)
    sub = lax.broadcasted_iota(I32, (ng, SUBLANES, lw), 1)
    d = 1
    while d < SUBLANES:
        keep = sub >= d
        a_sh = jnp.where(keep, pltpu.roll(a, d, 1), 1.0)
        b_sh = jnp.where(keep, pltpu.roll(b, d, 1), 0.0)
        b = a * b_sh + b
        a = a * a_sh
        d *= 2
    a_scr[...] = a
    b_scr[...] = b

    def carry(gidx, h):
        hg = a_scr[gidx] * h + b_scr[gidx]
        b_scr[gidx] = hg
        return hg[SUBLANES - 1:SUBLANES, :]

    h_last = lax.fori_loop(0, ng, carry, h_scr[...], unroll=8)
    h_scr[...] = h_last
    hs = b_scr[...].reshape(ts, lw)
    y_ref[0] = (hs * _gelu_tanh(gl_ref[0])).astype(y_ref.dtype)
    xs_scr[0:pad, :] = xs_scr[ts:ts + pad, :]

    @pl.when(tj == pl.num_programs(1) - 1)
    def _():
        ht_ref[0] = h_last
        nb_ref[0] = xs_scr[pad - (CONV_W - 1):pad, :]


def _lru_prompt(xl, gl, buf0, h0, cw, cb, wa_bd, ba, wx_bd, bx, lam):
    b, s, lw = xl.shape
    ts = _token_tile(s)
    seq = lambda i, j: (i, j, 0)
    per_b = lambda i, j: (i, 0, 0)
    const = lambda i, j: (0, 0)
    row = pl.BlockSpec((1, lw), const)
    return pl.pallas_call(
        functools.partial(_lru_prompt_kernel, ts=ts),
        out_shape=[jax.ShapeDtypeStruct((b, s, lw), MXU_DTYPE),
                   jax.ShapeDtypeStruct((b, 1, lw), F32),
                   jax.ShapeDtypeStruct((b, CONV_W - 1, lw), F32)],
        grid=(b, s // ts),
        in_specs=[pl.BlockSpec((1, ts, lw), seq), pl.BlockSpec((1, ts, lw), seq),
                  pl.BlockSpec((1, CONV_W - 1, lw), per_b), pl.BlockSpec((1, 1, lw), per_b),
                  pl.BlockSpec((CONV_W, lw), const), row,
                  pl.BlockSpec((lw, lw), const), row, pl.BlockSpec((lw, lw), const), row, row],
        out_specs=[pl.BlockSpec((1, ts, lw), seq), pl.BlockSpec((1, 1, lw), per_b),
                   pl.BlockSpec((1, CONV_W - 1, lw), per_b)],
        scratch_shapes=[pltpu.VMEM((ts + SUBLANES, lw), F32),
                        pltpu.VMEM((ts // SUBLANES, SUBLANES, lw), F32),
                        pltpu.VMEM((ts // SUBLANES, SUBLANES, lw), F32),
                        pltpu.VMEM((1, lw), F32)],
        compiler_params=_cparams(2),
        name="rglru_prompt",
    )(xl, gl, buf0, h0.reshape(b, 1, lw), cw, cb, wa_bd, ba, wx_bd, bx, lam)


def _lru_sample_kernel(xl_ref, gl_ref, buf_ref, h0_ref, cw_ref, cb_ref, wa_ref, ba_ref, wx_ref, bx_ref, lam_ref,
                       y_ref, ht_ref, nb_ref, *, ds):
    rows = [buf_ref[j] for j in range(CONV_W - 1)] + [xl_ref[t] for t in range(ds)]
    xcs = []
    for t in range(ds):
        xc = cb_ref[...]
        for j in range(CONV_W):
            xc = xc + rows[t + j] * cw_ref[j:j + 1, :]
        xcs.append(xc)
    db = xcs[0].shape[0]
    a, b = _lru_coeffs(jnp.concatenate(xcs, axis=0), wa_ref, ba_ref, wx_ref, bx_ref, lam_ref)
    h = h0_ref[...]
    for t in range(ds):
        h = a[t * db:(t + 1) * db] * h + b[t * db:(t + 1) * db]
        y_ref[t] = (h * _gelu_tanh(gl_ref[t])).astype(y_ref.dtype)
    ht_ref[...] = h
    for j in range(CONV_W - 1):
        nb_ref[j] = rows[ds + j]


def _lru_sample(xl_t, gl_t, buf_t, h0, cw, cb, wa_bd, ba, wx_bd, bx, lam):
    ds, db, lw = xl_t.shape
    full = lambda a: pl.BlockSpec(a.shape, lambda i, n=a.ndim: (0,) * n)
    args = (xl_t, gl_t, buf_t, h0, cw, cb, wa_bd, ba, wx_bd, bx, lam)
    outs = [jax.ShapeDtypeStruct((ds, db, lw), MXU_DTYPE), jax.ShapeDtypeStruct((db, lw), F32),
            jax.ShapeDtypeStruct((CONV_W - 1, db, lw), F32)]
    return pl.pallas_call(
        functools.partial(_lru_sample_kernel, ds=ds),
        out_shape=outs,
        grid=(1,),
        in_specs=[full(a) for a in args],
        out_specs=[full(o) for o in outs],
        compiler_params=_cparams(1),
        name="rglru_sample",
    )(*args)


def _key_to_float(key):
    key = jnp.maximum(key, jnp.int32(KEY_NEG_INF))
    bits = jnp.where(key < 0, key ^ jnp.int32(0x7FFFFFFF), key)
    return lax.bitcast_convert_type(bits, F32)


def _kth_largest_key(count_ge, shape, n_sel):
    def bit_body(i, t_u):
        cand_u = t_u | jnp.left_shift(jnp.int32(1), 31 - i)
        cnt = count_ge(cand_u ^ jnp.int32(INT_MIN))
        return jnp.where(cnt >= n_sel, cand_u, t_u)

    t_u = lax.fori_loop(0, 32, bit_body, jnp.zeros(shape, I32))
    return t_u ^ jnp.int32(INT_MIN)


def _kth_largest_key_radix4(count_ge3, shape, n_sel):
    int_min = jnp.int32(INT_MIN)

    def step(i, t_u):
        sh = 30 - 2 * i
        c1, c2, c3 = (t_u | jnp.left_shift(jnp.int32(d), sh) for d in (1, 2, 3))
        n1, n2, n3 = count_ge3(c1 ^ int_min, c2 ^ int_min, c3 ^ int_min)
        return jnp.where(n3 >= n_sel, c3, jnp.where(n2 >= n_sel, c2, jnp.where(n1 >= n_sel, c1, t_u)))

    t_u = lax.fori_loop(0, 16, step, jnp.zeros(shape, I32))
    return t_u ^ int_min


def _tie_cutoff(count_tie_below, need, shape, n_bits):
    def bit_body(i, j_u):
        cand = j_u | jnp.left_shift(jnp.int32(1), n_bits - 1 - i)
        return jnp.where(count_tie_below(cand) < need, cand, j_u)

    return lax.fori_loop(0, n_bits, bit_body, jnp.zeros(shape, I32))


def _mask_add(key, idx, t, jcut):
    tie = jnp.where(idx <= jcut, 0.0, NEG_INF)
    return jnp.where(key > t, 0.0, jnp.where(key == t, tie, NEG_INF))


def _fold_rows(x, group, op):
    slabs = [x[i * group:(i + 1) * group] for i in range(x.shape[0] // group)]
    chains = slabs[:4]
    for i, slab in enumerate(slabs[4:]):
        chains[i % len(chains)] = op(chains[i % len(chains)], slab)
    while len(chains) > 1:
        chains = [op(chains[i], chains[i + 1]) if i + 1 < len(chains) else chains[i]
                  for i in range(0, len(chains), 2)]
    return chains[0]


def _attn_prompt_kernel(qp_ref, qip_ref, wit_ref, ki2_ref, kb_ref, vbt_ref, o_ref,
                        key_scr, hi_scr, lo_scr, s0_scr, s1_scr, mx0_scr, mx1_scr, ls_scr, acc_scr, jc_scr,
                        *, n_sel, n_bits, kc, qb):
    s_scr, mx_scr = (s0_scr, s1_scr), (mx0_scr, mx1_scr)
    jb = pl.program_id(1)
    nch = ((jb + 1) * qb - 1) // kc + 1
    hpg = N_HEADS // N_KV
    n = hpg * qb

    def key_chunk(ref, c):
        return ref[0, pl.ds(pl.multiple_of(c * kc, kc), kc), :]

    def key_index(c):
        return c * kc + lax.broadcasted_iota(I32, (kc, qb), 0)

    qpos = jb * qb + lax.broadcasted_iota(I32, (kc, qb), 1)

    qi_all = qip_ref[0]
    qi_stack = jnp.concatenate([qi_all[:, h * LANES:(h + 1) * LANES] for h in range(IDX_HEADS)], axis=0)
    wt = wit_ref[...]

    def score_chunk(c, causal_edge):
        d = _dot_nt(key_chunk(ki2_ref, c), qi_stack)
        sc = jnp.maximum(d[:, 0:qb], 0.0) * wt[0:1, :]
        for h in range(1, IDX_HEADS):
            sc = sc + jnp.maximum(d[:, h * qb:(h + 1) * qb], 0.0) * wt[h:h + 1, :]
        if causal_edge:
            sc = jnp.where(key_index(c) > qpos, NEG_INF, sc)
        key_scr[c] = sc

    n_full = (jb * qb) // kc

    def full_body(c, carry):
        score_chunk(c, False)
        return carry

    def edge_body(c, carry):
        score_chunk(c, True)
        return carry

    lax.fori_loop(0, n_full, full_body, 0)
    lax.fori_loop(n_full, nch, edge_body, 0)

    def rows16(x):
        return jnp.broadcast_to(x, (PACKED_SUBLANES, qb)).astype(I16)

    def count(ind, ref):
        def body(c, acc):
            return acc + _fold_rows(jnp.where(ind(key_scr[c], ref), 1.0, 0.0), SUBLANES, jnp.add)

        acc = lax.fori_loop(0, nch, body, jnp.zeros((SUBLANES, qb), F32))
        return jnp.sum(acc, axis=0, keepdims=True).astype(I32)

    one16, zero16 = jnp.int16(1), jnp.int16(0)
    t = _key_to_float(_kth_largest_key(lambda cand: count(lambda s, c: s >= c, _key_to_float(cand)), (1, qb), n_sel))
    cnt_gt = count(lambda s, c: s > c, t)
    cnt_ge = count(lambda s, c: s >= c, t)
    need = n_sel - cnt_gt
    live = t > NEG_INF
    excess = jnp.where(jnp.logical_and(live, cnt_ge > n_sel), 1.0, 0.0)
    jc_scr[...] = jnp.broadcast_to(jnp.where(live, jnp.int32(2 ** 30), jnp.int32(-1)), jc_scr.shape)

    @pl.when(jnp.max(excess) > 0.0)
    def _():
        def mark_body(c, carry):
            lo_scr[c] = jnp.where(key_scr[c] == t, 1, 0).astype(I16)
            return carry

        lax.fori_loop(0, nch, mark_body, 0)
        row16 = lax.broadcasted_iota(I32, (PACKED_SUBLANES, qb), 0).astype(I16)

        def ties_below(cand):
            def body(c, accs):
                v = lo_scr[c]
                rel = rows16(cand - c * kc)
                accs = list(accs)
                for i in range(kc // PACKED_SUBLANES):
                    j = i % len(accs)
                    below = (row16 + jnp.int16(i * PACKED_SUBLANES)) < rel
                    accs[j] = accs[j] + jnp.where(below, v[i * PACKED_SUBLANES:(i + 1) * PACKED_SUBLANES], zero16)
                return tuple(accs)

            zero = jnp.zeros((PACKED_SUBLANES, qb), I16)
            accs = lax.fori_loop(0, nch, body, (zero, zero, zero, zero))
            acc = (accs[0] + accs[1]) + (accs[2] + accs[3])
            return jnp.sum(acc.astype(I32), axis=0, keepdims=True)

        jcut = _tie_cutoff(ties_below, need, (1, qb), n_bits)
        jc_scr[...] = jnp.broadcast_to(jnp.where(live, jcut, jnp.int32(-1)), jc_scr.shape)

    jcut = jc_scr[0:1, :]

    def logit_step(g, qg, c):
        s = _dot_nt(key_chunk(kb_ref, c), qg)
        if g == 0:
            m_add = _mask_add(key_scr[c], key_index(c), t, jcut)
            hi_scr[c] = pltpu.bitcast(m_add.astype(jnp.bfloat16), I16)
        else:
            m_add = pltpu.bitcast(hi_scr[c], jnp.bfloat16).astype(F32)
        for i in range(hpg):
            cols = slice(i * qb, (i + 1) * qb)
            si = s[:, cols] + m_add
            s_scr[g][c, :, cols] = si
            mx_scr[g][:, cols] = jnp.maximum(mx_scr[g][:, cols], _fold_rows(si, SUBLANES, jnp.maximum))

    def value_step(g, m, c):
        p = jnp.exp2(s_scr[g][c] - m)
        ls_scr[...] += _fold_rows(p, SUBLANES, jnp.add)
        acc_scr[...] += _dot(vbt_ref[0, c], p.astype(MXU_DTYPE))

    def chunk_loop(step):
        def body(i, carry):
            step(2 * i)
            step(2 * i + 1)
            return carry
        lax.fori_loop(0, nch // 2, body, 0)

        @pl.when(nch % 2 == 1)
        def _():
            step(nch - 1)

    qgs = [jnp.concatenate([qp_ref[0, :, (g * hpg + i) * LANES:(g * hpg + i + 1) * LANES] for i in range(hpg)],
                           axis=0) for g in range(N_KV)]
    for mx in mx_scr:
        mx[...] = jnp.full(mx.shape, NEG_INF, F32)
    chunk_loop(lambda c: logit_step(0, qgs[0], c))
    for g in range(N_KV):
        m = jnp.max(mx_scr[g][...], axis=0, keepdims=True)
        ls_scr[...] = jnp.zeros(ls_scr.shape, F32)
        acc_scr[...] = jnp.zeros(acc_scr.shape, F32)
        if g + 1 < N_KV:
            def both(c, g=g, m=m):
                value_step(g, m, c)
                logit_step(g + 1, qgs[g + 1], c)
            chunk_loop(both)
        else:
            chunk_loop(lambda c, g=g, m=m: value_step(g, m, c))
        o_t = acc_scr[...] / jnp.sum(ls_scr[...], axis=0, keepdims=True)
        feats = slice(g * HEAD_DIM, (g + 1) * HEAD_DIM)
        for pair in range(hpg // 2):
            even = o_t[feats, (2 * pair) * qb:(2 * pair + 1) * qb]
            odd = o_t[feats, (2 * pair + 1) * qb:(2 * pair + 2) * qb]
            tile = g * (hpg // 2) + pair
            o_ref[0, :, tile * LANES:(tile + 1) * LANES] = jnp.concatenate([even, odd], axis=0).T.astype(o_ref.dtype)


def _attn_prompt(qp, qip, wit, ki2, kb, vbt, n_sel):
    b, s, _ = qp.shape
    kc = vbt.shape[3]
    qb = ATT_QUERIES if s % ATT_QUERIES == 0 else Q_BLOCK
    nq = s // qb
    n_bits = max(1, (s - 1).bit_length())
    blk = lambda i, j: (i, j, 0)
    seq = lambda i, j: (i, 0, 0)
    n = (N_HEADS // N_KV) * qb
    return pl.pallas_call(
        functools.partial(_attn_prompt_kernel, n_sel=n_sel, n_bits=n_bits, kc=kc, qb=qb),
        out_shape=jax.ShapeDtypeStruct((b, s, N_HEADS * HEAD_DIM), MXU_DTYPE),
        grid=(b, nq),
        in_specs=[pl.BlockSpec((1, qb, qp.shape[2]), blk),
                  pl.BlockSpec((1, qb, qip.shape[2]), blk),
                  pl.BlockSpec((SUBLANES, qb), lambda i, j: (0, i * nq + j)),
                  pl.BlockSpec((1, s, LANES), seq),
                  pl.BlockSpec((1, s, LANES), seq),
                  pl.BlockSpec((1, s // kc, LANES, kc), lambda i, j: (i, 0, 0, 0))],
        out_specs=pl.BlockSpec((1, qb, N_HEADS * HEAD_DIM), blk),
        scratch_shapes=[pltpu.VMEM((s // kc, kc, qb), F32),
                        pltpu.VMEM((s // kc, kc, qb), I16),
                        pltpu.VMEM((s // kc, kc, qb), I16),
                        pltpu.VMEM((s // kc, kc, n), F32),
                        pltpu.VMEM((s // kc, kc, n), F32),
                        pltpu.VMEM((SUBLANES, n), F32),
                        pltpu.VMEM((SUBLANES, n), F32),
                        pltpu.VMEM((SUBLANES, n), F32),
                        pltpu.VMEM((LANES, n), F32),
                        pltpu.VMEM((SUBLANES, qb), I32)],
        compiler_params=_cparams(2),
        name="attn_prompt",
    )(qp, qip, wit, ki2, kb, vbt)


def _attn_sample_kernel(pt_ref, qs_ref, qis_ref, wcol_ref, kin_ref, kn_ref, vn_ref, ckidx_ref, ck_ref, cv_ref,
                        o_ref, kidx_buf, k_buf, v_buf, key_scr, madd_scr, s_scr, jc_scr, sems,
                        *, layer, n_pages, page, ds, n_sel, n_bits, kc):
    bi = pl.program_id(0)
    past = n_pages * page
    total = past + LANES
    n_chunks = past // kc
    slot = bi % 2

    streams = ((ckidx_ref, kidx_buf), (ck_ref, k_buf), (cv_ref, v_buf))

    def page_copy(which, seq, slot_, p):
        src, dst = streams[which]
        cols = pl.ds(pl.multiple_of(p * page, page), page)
        return pltpu.make_async_copy(src.at[layer, pt_ref[seq, p]], dst.at[slot_, :, cols], sems.at[slot_, which])

    def fetch(seq, slot_):
        def body(p, carry):
            for which in range(len(streams)):
                page_copy(which, seq, slot_, p).start()
            return carry
        lax.fori_loop(0, n_pages, body, 0)

    def wait_stream(which):
        dst = streams[which][1].at[slot, :, 0:past]
        pltpu.make_async_copy(dst, dst, sems.at[slot, which]).wait()

    @pl.when(bi == 0)
    def _():
        fetch(bi, slot)

    @pl.when(bi + 1 < pl.num_programs(0))
    def _():
        fetch(bi + 1, 1 - slot)

    kidx_buf[slot, :, past:total] = kin_ref[0]
    k_buf[slot, :, past:total] = kn_ref[0]
    v_buf[slot, :, past:total] = vn_ref[0]
    key_scr[...] = jnp.full(key_scr.shape, NEG_INF, F32)

    qis = qis_ref[0]
    wcol = wcol_ref[0][:, 0:1]
    wait_stream(0)

    def scores(lo_col, width):
        d = _dot(qis, kidx_buf[slot, :, lo_col:lo_col + width].astype(MXU_DTYPE))
        wgt = jnp.maximum(d, 0.0) * wcol
        sc = wgt[0:ds]
        for h in range(1, IDX_HEADS):
            sc = sc + wgt[h * ds:(h + 1) * ds]
        return sc

    top = SUBLANES // 2
    half = past // 2
    hw = key_scr.shape[1]

    def place(lo_col):
        return (0, lo_col) if lo_col < half else (top, lo_col - half)

    for c in range(n_chunks):
        r0, c0 = place(c * kc)
        key_scr[r0:r0 + ds, c0:c0 + kc] = scores(c * kc, kc)
    sc_new = scores(past, LANES)
    lane_new = lax.broadcasted_iota(I32, (ds, LANES), 1)
    row_new = lax.broadcasted_iota(I32, (ds, LANES), 0)
    r0, c0 = place(past)
    key_scr[r0:r0 + ds, c0:c0 + LANES] = jnp.where(lane_new > row_new, NEG_INF, sc_new)

    rows = key_scr.shape[0]
    lane = lax.broadcasted_iota(I32, (rows, LANES), 1)
    col0 = lane + jnp.where(lax.broadcasted_iota(I32, (rows, LANES), 0) >= top, half, 0)

    def count(ind, *row_args):
        args = [jnp.broadcast_to(a, (rows, LANES)) for a in row_args]
        accs = None
        for i in range(hw // LANES):
            vals = ind(key_scr[:, i * LANES:(i + 1) * LANES], col0 + i * LANES, *args)
            accs = vals if accs is None else tuple(a + v for a, v in zip(accs, vals))
        return tuple(jnp.sum(a + pltpu.roll(a, top, 0), axis=1, keepdims=True) for a in accs)

    def count_ge3(c1, c2, c3):
        return count(lambda k, col, b1, b2, b3: (jnp.where(k >= b1, 1.0, 0.0), jnp.where(k >= b2, 1.0, 0.0),
                                                 jnp.where(k >= b3, 1.0, 0.0)),
                     _key_to_float(c1), _key_to_float(c2), _key_to_float(c3))

    t = _key_to_float(_kth_largest_key_radix4(count_ge3, (rows, 1), n_sel))
    cnt_gt, cnt_ge = count(lambda k, col, tb: (jnp.where(k > tb, 1.0, 0.0), jnp.where(k >= tb, 1.0, 0.0)), t)
    need = n_sel - cnt_gt
    live = t > NEG_INF
    excess = jnp.where(jnp.logical_and(live, cnt_ge > n_sel), 1.0, 0.0)
    jc_scr[...] = jnp.broadcast_to(jnp.where(live, jnp.int32(2 ** 30), jnp.int32(-1)), jc_scr.shape)

    @pl.when(jnp.max(excess) > 0.0)
    def _():
        def ties_below(cand):
            return count(lambda k, col, tb, cb: (jnp.where(k == tb, jnp.where(col < cb, 1.0, 0.0), 0.0),), t, cand)[0]

        jcut = _tie_cutoff(ties_below, need, (rows, 1), n_bits)
        jc_scr[...] = jnp.broadcast_to(jnp.where(live, jcut, jnp.int32(-1)), jc_scr.shape)

    tb, jb = jnp.broadcast_to(t, (rows, LANES)), jc_scr[...]
    for i in range(hw // LANES):
        cols = slice(i * LANES, (i + 1) * LANES)
        madd_scr[:, cols] = _mask_add(key_scr[:, cols], col0 + i * LANES, tb, jb)

    qs = qs_ref[0]
    nrow = qs.shape[0]

    def row_mask(lo_col, width):
        r0, c0 = place(lo_col)
        return jnp.concatenate(
            [jnp.broadcast_to(madd_scr[r0 + tt:r0 + tt + 1, c0:c0 + width], (N_HEADS, width)) for tt in range(ds)],
            axis=0)

    spans = [(c * kc, kc) for c in range(n_chunks)] + [(past, LANES)]
    wait_stream(1)
    m = jnp.full((nrow, 1), NEG_INF, F32)
    for lo_col, width in spans:
        s = _dot(qs, k_buf[slot, :, lo_col:lo_col + width].astype(MXU_DTYPE)) + row_mask(lo_col, width)
        s_scr[:, lo_col:lo_col + width] = s
        m = jnp.maximum(m, jnp.max(s, axis=1, keepdims=True))
    wait_stream(2)
    l = jnp.zeros((nrow, 1), F32)
    acc = jnp.zeros((nrow, LANES), F32)
    for lo_col, width in spans:
        p = jnp.exp2(s_scr[:, lo_col:lo_col + width] - m)
        l = l + jnp.sum(p, axis=1, keepdims=True)
        acc = acc + _dot_nt(p.astype(MXU_DTYPE), v_buf[slot, :, lo_col:lo_col + width].astype(MXU_DTYPE))
    o_ref[0] = acc / l


def _attn_sample(page_table, qs, qis, wcol, ki_new, k_new, v_new, cache_kidx, cache_k2, cache_v2, layer, n_sel, ds):
    db, n_pages = page_table.shape
    page = cache_kidx.shape[3]
    past = n_pages * page
    total = past + LANES
    kc = 2048 if past % 4096 == 0 else page
    assert (past // kc) % 2 == 0 and 2 * ds <= SUBLANES
    n_bits = max(1, (total - 1).bit_length())
    per = lambda i, pt: (i, 0, 0)
    any_spec = pl.BlockSpec(memory_space=pl.ANY)
    grid_spec = pltpu.PrefetchScalarGridSpec(
        num_scalar_prefetch=1,
        grid=(db,),
        in_specs=[pl.BlockSpec((1,) + qs.shape[1:], per), pl.BlockSpec((1,) + qis.shape[1:], per),
                  pl.BlockSpec((1,) + wcol.shape[1:], per), pl.BlockSpec((1,) + ki_new.shape[1:], per),
                  pl.BlockSpec((1,) + k_new.shape[1:], per), pl.BlockSpec((1,) + v_new.shape[1:], per),
                  any_spec, any_spec, any_spec],
        out_specs=pl.BlockSpec((1, qs.shape[1], LANES), per),
        scratch_shapes=[pltpu.VMEM((2, IDX_DIM, total), F32),
                        pltpu.VMEM((2, LANES, total), F32),
                        pltpu.VMEM((2, LANES, total), F32),
                        pltpu.VMEM((SUBLANES, past // 2 + LANES), F32),
                        pltpu.VMEM((SUBLANES, past // 2 + LANES), F32),
                        pltpu.VMEM((qs.shape[1], total), F32),
                        pltpu.VMEM((SUBLANES, LANES), I32),
                        pltpu.SemaphoreType.DMA((2, 3))])
    return pl.pallas_call(
        functools.partial(_attn_sample_kernel, layer=layer, n_pages=n_pages, page=page, ds=ds, n_sel=n_sel,
                          n_bits=n_bits, kc=kc),
        out_shape=jax.ShapeDtypeStruct((db, qs.shape[1], LANES), F32),
        grid_spec=grid_spec,
        compiler_params=_cparams(1),
        name="attn_sample",
    )(page_table, qs, qis, wcol, ki_new, k_new, v_new, cache_kidx, cache_k2, cache_v2)


def _rope_tables(pos):
    half = ROPE_DIM // 2
    freqs = ROPE_THETA ** (-(jnp.arange(half, dtype=F32) / half))
    ang = pos.astype(F32)[:, None] * freqs[None, :]
    cos, sin = jnp.cos(ang), jnp.sin(ang)
    rest = HEAD_DIM - ROPE_DIM
    c64 = jnp.concatenate([cos, cos, jnp.ones((pos.shape[0], rest), F32)], axis=1)
    s64 = jnp.concatenate([-sin, sin, jnp.zeros((pos.shape[0], rest), F32)], axis=1)
    return jnp.tile(c64, (1, LANES // HEAD_DIM)), jnp.tile(s64, (1, LANES // HEAD_DIM))


def _block_diag(w):
    nb, n, _ = w.shape
    eye = jnp.eye(nb, dtype=w.dtype)
    return (eye[:, None, :, None] * w[:, :, None, :]).reshape(nb * n, nb * n)


def kernel(x_prompt, x_sample, c_prompt, c_sample, cache_k, cache_v, cache_kidx, state_h, state_conv, page_table,
           w_ada, b_ada, norm_g, ffn_w_gu, ffn_w_down, w_in, conv_w, conv_b, lru_wa, lru_ba, lru_wx, lru_bx,
           lru_lambda, w_out):
    b, s, d = x_prompt.shape
    db, ds, _ = x_sample.shape
    depth = w_ada.shape[0]
    lw = lru_lambda.shape[1]
    n_pool, page = cache_k.shape[1], cache_k.shape[2]
    past = page_table.shape[1] * page
    kv_w = N_KV * HEAD_DIM
    tm_p = _token_tile(b * s)
    assert kv_w == LANES and 2 * IDX_DIM == LANES and s % tm_p == 0 and tm_p % Q_BLOCK == 0 and page % LANES == 0

    cos_p, sin_p = _rope_tables(jnp.arange(s, dtype=I32))
    cos_s, sin_s = _rope_tables(past + jnp.arange(ds, dtype=I32))
    cos_s, sin_s = jnp.tile(cos_s, (db, 1)), jnp.tile(sin_s, (db, 1))

    n_c = b + db
    c_all = jnp.concatenate([c_prompt, c_sample, jnp.zeros((-n_c % SUBLANES, d), F32)], axis=0)
    cache_kidx_t = cache_kidx.transpose(0, 1, 3, 2)
    cache_k_t = cache_k.transpose(0, 1, 3, 4, 2).reshape(depth, n_pool, kv_w, page)
    cache_v_t = cache_v.transpose(0, 1, 3, 4, 2).reshape(depth, n_pool, kv_w, page)

    def new_cols(a):
        a = a.reshape(db, ds, a.shape[-1]).transpose(0, 2, 1)
        return jnp.pad(a, ((0, 0), (0, 0), (0, LANES - ds)))

    xp = x_prompt.reshape(b * s, d)
    xs = x_sample.reshape(db * ds, d)
    n_sel_p = min(TOP_K, s // 4)
    n_sel_s = min(TOP_K, (past + ds) // 4)
    q_w = N_HEADS * HEAD_DIM
    qi_w = IDX_HEADS * IDX_DIM
    col_ki = 2 * lw + q_w + 2 * kv_w + qi_w

    wgu_all = ffn_w_gu.astype(MXU_DTYPE)
    wdn_all = ffn_w_down.astype(MXU_DTYPE)
    wo_all = w_out.astype(MXU_DTYPE)
    w_ki = w_in[:, :, col_ki:col_ki + IDX_DIM]
    w_pad_all = jnp.concatenate(
        [w_in[:, :, :col_ki], w_ki, w_ki, w_in[:, :, col_ki + IDX_DIM:],
         jnp.zeros((depth, d, LANES - IDX_HEADS), F32)], axis=2).astype(MXU_DTYPE)

    outs = {name: [] for name in ("kp", "vp", "kip", "hp", "cp", "ks", "vs", "kis", "hs", "cs")}
    for l in range(depth):
        mod = _ada_mod(c_all, w_ada, b_ada, l)
        mod_p = mod[:b].reshape(b, N_MOD, 1, d)
        mod_s = jnp.repeat(mod[b:n_c].reshape(db, N_MOD, d), ds, axis=0).transpose(1, 0, 2)[None]
        wa_bd = _block_diag(lru_wa[l]).astype(MXU_DTYPE)
        wx_bd = _block_diag(lru_wx[l]).astype(MXU_DTYPE)
        lru_args = (conv_w[l], conv_b[l].reshape(1, lw), wa_bd, lru_ba[l].reshape(1, lw), wx_bd,
                    lru_bx[l].reshape(1, lw), lru_lambda[l].reshape(1, lw))

        xp = _ffn(xp, mod_p, norm_g, wgu_all, wdn_all, l, 0, 0, 0)
        xl, gl, qp, k_t, v_t, kb, vbt, qip, ki_t, ki2, _, wit = _proj(xp, mod_p, norm_g, w_pad_all, l, cos_p, sin_p,
                                                                      lw, s // tm_p)
        r3 = lambda a: a.reshape(b, s, a.shape[-1])
        y_lru, h_t, nbuf = _lru_prompt(r3(xl), r3(gl), jnp.zeros((b, CONV_W - 1, lw), F32), jnp.zeros((b, lw), F32),
                                       *lru_args)
        y_att = _attn_prompt(r3(qp), r3(qip), wit, r3(ki2), r3(kb), vbt.reshape(b, s // tm_p, LANES, tm_p), n_sel_p)
        xp = _out_ffn(xp, y_lru.reshape(b * s, lw), y_att.reshape(b * s, q_w), mod_p, norm_g, wo_all, wgu_all, wdn_all, l)
        outs["kp"].append(k_t.transpose(0, 2, 1).reshape(b, s, N_KV, HEAD_DIM))
        outs["vp"].append(v_t.transpose(0, 2, 1).reshape(b, s, N_KV, HEAD_DIM))
        outs["kip"].append(ki_t.transpose(0, 2, 1))
        outs["hp"].append(h_t.reshape(b, lw))
        outs["cp"].append(nbuf)

        xs = _ffn(xs, mod_s, norm_g, wgu_all, wdn_all, l, 0, 0, 0)
        xl, gl, qp, k_t, v_t, _, _, qip, ki_t, _, wi, _ = _proj(xs, mod_s, norm_g, w_pad_all, l, cos_s, sin_s, lw, 1)
        k, v, ki = k_t[0].T, v_t[0].T, ki_t[0].T
        tmaj = lambda a: a.reshape(db, ds, a.shape[-1]).transpose(1, 0, 2)
        y_lru, h_t, nbuf = _lru_sample(tmaj(xl), tmaj(gl), state_conv[l].transpose(1, 0, 2), state_h[l], *lru_args)
        qs = qp.reshape(db, ds * N_HEADS, LANES)
        qis = qip.reshape(db, ds, IDX_HEADS, 2, IDX_DIM)
        qis = jnp.stack([qis[:, :, h, h % 2] for h in range(IDX_HEADS)], axis=1).reshape(db, IDX_HEADS * ds, IDX_DIM)
        wcol = wi.reshape(db, ds, LANES)[:, :, :IDX_HEADS].transpose(0, 2, 1).reshape(db, IDX_HEADS * ds, 1)
        wcol = jnp.broadcast_to(wcol, (db, IDX_HEADS * ds, LANES))
        o_s = _attn_sample(page_table, qs, qis, wcol, new_cols(ki), new_cols(k), new_cols(v),
                           cache_kidx_t, cache_k_t, cache_v_t, l, n_sel_s, ds)
        o_s = o_s.reshape(db, ds, N_HEADS, N_KV, HEAD_DIM)
        hpg = N_HEADS // N_KV
        y_att = jnp.stack([o_s[:, :, h, h // hpg] for h in range(N_HEADS)], axis=2).reshape(db * ds, q_w)
        xs = _out_ffn(xs, y_lru.transpose(1, 0, 2).reshape(db * ds, lw), y_att.astype(MXU_DTYPE), mod_s, norm_g, wo_all,
                      wgu_all, wdn_all, l)
        outs["ks"].append(k.reshape(db, ds, N_KV, HEAD_DIM))
        outs["vs"].append(v.reshape(db, ds, N_KV, HEAD_DIM))
        outs["kis"].append(ki.reshape(db, ds, IDX_DIM))
        outs["hs"].append(h_t)
        outs["cs"].append(nbuf.transpose(1, 0, 2))

    st = lambda name: jnp.stack(outs[name])
    return (xp.reshape(b, s, d), xs.reshape(db, ds, d), st("kp"), st("vp"), st("kip"), st("hp"), st("cp"),
            st("ks"), st("vs"), st("kis"), st("hs"), st("cs"))
```
